```python
import math
import jax
import jax.numpy as jnp
from jax import lax
import numpy as np

D_MODEL = 2048
BATCH = 1
SEQ = 8192
DEPTH = 2

EPS = 1e-6
Q_BLOCK = 128
NEG_INF = -1e30
FORCE = 1e9
NSA_HEADS = 16
NSA_GROUPS = 2
NSA_HPG = NSA_HEADS // NSA_GROUPS
NSA_DK = 64
NSA_DV = 64
NSA_CMP_HID = 64
CMP_LEN = 32
CMP_STRIDE = 16
SLC_LEN = 64
SLC_TOPK = 16
WINDOW = 512
MLA_HEADS = 8
Q_LORA = 512
KV_LORA = 512
QK_NOPE = 128
QK_ROPE = 64
V_HEAD = 128
ROPE_THETA = 10000.0
RWKV_HEAD = 64
RWKV_HEADS = D_MODEL // RWKV_HEAD
DECAY_LORA = 96
A_LORA = 96
LNX_EPS = 64e-5
NSA_Q_W = NSA_HEADS * NSA_DK
NSA_KV_W = 6 * NSA_GROUPS * NSA_DK
NSA_G_W = 3 * NSA_HEADS
NSA_Z_W = NSA_HEADS * NSA_DV
MLA_QA_W = Q_LORA
MLA_KVA_W = KV_LORA + QK_ROPE
MLA_Z_W = MLA_HEADS * V_HEAD
IN0_SIZES = (NSA_Q_W, NSA_KV_W, NSA_G_W, NSA_Z_W, MLA_QA_W, MLA_KVA_W, MLA_Z_W)
IN0_W = NSA_Q_W + NSA_KV_W + NSA_G_W + NSA_Z_W + MLA_QA_W + MLA_KVA_W + MLA_Z_W
MIX0_W = NSA_HEADS * NSA_DV + MLA_HEADS * V_HEAD

kernel_name = "nsa_mla_rwkv7_adaln_hybrid"


def rmsnorm(x, g):
    xf = x.astype(jnp.float32)
    y = xf * lax.rsqrt(jnp.mean(xf * xf, axis=-1, keepdims=True) + EPS)
    return (y * g.astype(jnp.float32)).astype(x.dtype)


def masked_softmax(s, mask, axis=-1):
    s = jnp.where(mask, s.astype(jnp.float32), NEG_INF)
    p = jax.nn.softmax(s, axis=axis)
    return jnp.where(mask, p, 0.0)


def alibi_slopes(n):
    return 2.0 ** (-8.0 * jnp.arange(1, n + 1, dtype=jnp.float32) / n)


def rope_angles(S, dim):
    inv = ROPE_THETA ** (-jnp.arange(0, dim, 2, dtype=jnp.float32) / dim)
    ang = jnp.arange(S, dtype=jnp.float32)[:, None] * inv[None]
    return jnp.cos(ang), jnp.sin(ang)


def apply_rope(x, cos, sin):
    x1, x2 = jnp.split(x.astype(jnp.float32), 2, axis=-1)
    extra = x.ndim - 3
    cos = cos.reshape(cos.shape[0], *([1] * extra), cos.shape[-1])
    sin = sin.reshape(sin.shape[0], *([1] * extra), sin.shape[-1])
    return jnp.concatenate([x1 * cos - x2 * sin, x1 * sin + x2 * cos], axis=-1).astype(x.dtype)


def split_cols(x, sizes):
    offs, acc = [], 0
    for s in sizes[:-1]:
        acc += s
        offs.append(acc)
    return jnp.split(x, offs, axis=-1)


def nsa_attention(q, kc, vc, ks, vs, kw, vw, gates,
                  pe_k, w1_k, b1_k, w2_k, pe_v, w1_v, b1_v, w2_v):
    B, S = q.shape[0], q.shape[1]
    G, HPG, DK = NSA_GROUPS, NSA_HPG, NSA_DK
    n_cmp = (S - CMP_LEN) // CMP_STRIDE + 1
    n_slc = S // SLC_LEN
    top = min(SLC_TOPK, n_slc)
    idx = jnp.arange(n_cmp)[:, None] * CMP_STRIDE + jnp.arange(CMP_LEN)[None]

    def compress(t, pe, w1, b1, w2):
        blk = t[:, idx] + pe[None, None, :, None, :]
        hid = jax.nn.silu(jnp.einsum('bnlgd,lde->bnge', blk, w1) + b1)
        return jnp.einsum('bnge,ed->bngd', hid, w2)

    k_cmp = compress(kc, pe_k, w1_k, b1_k, w2_k)
    v_cmp = compress(vc, pe_v, w1_v, b1_v, w2_v)
    cmp_start = jnp.arange(n_cmp) * CMP_STRIDE
    cmp_end = cmp_start + CMP_LEN - 1
    slc_start = jnp.arange(n_slc) * SLC_LEN
    overlap = ((cmp_start[:, None] < slc_start[None] + SLC_LEN)
               & (cmp_start[:, None] + CMP_LEN > slc_start[None])).astype(jnp.float32)
    ks_blk = ks.reshape(B, n_slc, SLC_LEN, G, DK).transpose(0, 3, 1, 2, 4)
    vs_blk = vs.reshape(B, n_slc, SLC_LEN, G, DK).transpose(0, 3, 1, 2, 4)
    pad = ((0, 0), (WINDOW, 0), (0, 0), (0, 0))
    kw_pad = jnp.pad(kw, pad)
    vw_pad = jnp.pad(vw, pad)
    slopes = alibi_slopes(NSA_HEADS).reshape(G, HPG)
    scale = DK ** -0.5
    qg = q.reshape(B, S, G, HPG, DK)
    gg = gates.reshape(B, S, G, HPG, 3)
    bidx = jnp.arange(B)[:, None, None, None]
    gidx = jnp.arange(G)[None, :, None, None]
    jb = jnp.arange(n_slc)

    def block(nb):
        start = nb * Q_BLOCK
        qb = lax.dynamic_slice_in_dim(qg, start, Q_BLOCK, axis=1)
        gb = lax.dynamic_slice_in_dim(gg, start, Q_BLOCK, axis=1)
        t = start + jnp.arange(Q_BLOCK)
        tf = t.astype(jnp.float32)
        s = jnp.einsum('bqghd,bngd->bghqn', qb, k_cmp).astype(jnp.float32) * scale
        s = s - slopes[None, :, :, None, None] * (tf[:, None] - cmp_end[None].astype(jnp.float32))
        p_cmp = masked_softmax(s, cmp_end[None] <= t[:, None])
        o_cmp = jnp.einsum('bghqn,bngd->bqghd', p_cmp.astype(v_cmp.dtype), v_cmp)
        imp = jnp.einsum('bghqn,nj->bgqj', p_cmp, overlap)
        cur = t // SLC_LEN
        forced = (jb[None] == 0) | (jb[None] == cur[:, None]) | (jb[None] == cur[:, None] - 1)
        imp = jnp.where(forced, FORCE, imp)
        imp = jnp.where(slc_start[None] <= t[:, None], imp, NEG_INF)
        _, sel = lax.top_k(imp, top)
        k_sel = ks_blk[bidx, gidx, sel]
        v_sel = vs_blk[bidx, gidx, sel]
        pos = sel[..., None] * SLC_LEN + jnp.arange(SLC_LEN)
        dist = t[None, None, :, None, None] - pos
        s = jnp.einsum('bqghd,bgqnld->bghqnl', qb, k_sel).astype(jnp.float32) * scale
        s = s - slopes[None, :, :, None, None, None] * dist[:, :, None].astype(jnp.float32)
        p = masked_softmax(s, (dist >= 0)[:, :, None], axis=(-2, -1))
        o_slc = jnp.einsum('bghqnl,bgqnld->bqghd', p.astype(v_sel.dtype), v_sel)
        kwb = lax.dynamic_slice_in_dim(kw_pad, start, WINDOW + Q_BLOCK, axis=1)
        vwb = lax.dynamic_slice_in_dim(vw_pad, start, WINDOW + Q_BLOCK, axis=1)
        spos = start - WINDOW + jnp.arange(WINDOW + Q_BLOCK)
        d = t[:, None] - spos[None]
        s = jnp.einsum('bqghd,bkgd->bghqk', qb, kwb).astype(jnp.float32) * scale
        s = s - slopes[None, :, :, None, None] * d.astype(jnp.float32)
        mw = (d >= 0) & (d < WINDOW) & (spos[None] >= 0)
        p = masked_softmax(s, mw)
        o_win = jnp.einsum('bghqk,bkgd->bqghd', p.astype(vwb.dtype), vwb)
        return gb[..., 0:1] * o_cmp + gb[..., 1:2] * o_slc + gb[..., 2:3] * o_win

    out = lax.map(block, jnp.arange(S // Q_BLOCK))
    return out.transpose(1, 0, 2, 3, 4, 5).reshape(B, S, NSA_HEADS * NSA_DV)


def mla_attention(q_a, kv_a, qa_g, w_qb, kva_g, w_kvb):
    B, S = q_a.shape[0], q_a.shape[1]
    q = (rmsnorm(q_a, qa_g) @ w_qb).reshape(B, S, MLA_HEADS, QK_NOPE + QK_ROPE)
    q_nope, q_pe = q[..., :QK_NOPE], q[..., QK_NOPE:]
    c_kv, k_pe = kv_a[..., :KV_LORA], kv_a[..., KV_LORA:]
    kv = (rmsnorm(c_kv, kva_g) @ w_kvb).reshape(B, S, MLA_HEADS, QK_NOPE + V_HEAD)
    k_nope, v = kv[..., :QK_NOPE], kv[..., QK_NOPE:]
    cos, sin = rope_angles(S, QK_ROPE)
    q_pe = apply_rope(q_pe, cos, sin)
    k_pe = apply_rope(k_pe, cos, sin)
    scale = (QK_NOPE + QK_ROPE) ** -0.5
    kpos = jnp.arange(S)

    def block(nb):
        start = nb * Q_BLOCK
        qn = lax.dynamic_slice_in_dim(q_nope, start, Q_BLOCK, axis=1)
        qp = lax.dynamic_slice_in_dim(q_pe, start, Q_BLOCK, axis=1)
        t = start + jnp.arange(Q_BLOCK)
        s = (jnp.einsum('bqhd,bkhd->bhqk', qn, k_nope)
             + jnp.einsum('bqhd,bkd->bhqk', qp, k_pe)).astype(jnp.float32) * scale
        p = masked_softmax(s, kpos[None] <= t[:, None])
        return jnp.einsum('bhqk,bkhd->bqhd', p.astype(v.dtype), v)

    out = lax.map(block, jnp.arange(S // Q_BLOCK))
    return out.transpose(1, 0, 2, 3, 4).reshape(B, S, MLA_HEADS * V_HEAD)


def hybrid_attention_mixer(h, w_in, w_out, pe_k, w1_k, b1_k, w2_k, pe_v, w1_v, b1_v, w2_v,
                           qa_g, w_qb, kva_g, w_kvb):
    B, S, _ = h.shape
    proj = h @ w_in
    nsa_q, nsa_kv, nsa_g, nsa_z, mla_qa, mla_kva, mla_z = split_cols(proj, IN0_SIZES)
    q = nsa_q.reshape(B, S, NSA_HEADS, NSA_DK)
    kv6 = nsa_kv.reshape(B, S, 6, NSA_GROUPS, NSA_DK)
    gates = jax.nn.sigmoid(nsa_g).reshape(B, S, NSA_HEADS, 3)
    o_nsa = nsa_attention(q, kv6[:, :, 0], kv6[:, :, 1], kv6[:, :, 2], kv6[:, :, 3],
                          kv6[:, :, 4], kv6[:, :, 5], gates,
                          pe_k, w1_k, b1_k, w2_k, pe_v, w1_v, b1_v, w2_v)
    o_mla = mla_attention(mla_qa, mla_kva, qa_g, w_qb, kva_g, w_kvb)
    y = jnp.concatenate([o_nsa * jax.nn.silu(nsa_z), o_mla * jax.nn.silu(mla_z)], axis=-1)
    return y @ w_out


def rwkv7_mixer(h, mu, w_r, w_k, w_v, w_z, w_o, w0, w1, w2, a0, a1, a2, k_k, k_a, r_k, lnx_g, lnx_b):
    B, S, D = h.shape
    H, N = RWKV_HEADS, RWKV_HEAD
    xx = jnp.pad(h, ((0, 0), (1, 0), (0, 0)))[:, :-1] - h
    xr, xw, xk, xv, xa, xz = [h + xx * mu[i] for i in range(6)]
    r = xr @ w_r
    k = xk @ w_k
    v = xv @ w_v
    z = xz @ w_z
    w = -jax.nn.softplus(-(w0 + jnp.tanh(xw @ w1) @ w2)) - 0.5
    decay = jnp.exp(-jnp.exp(w.astype(jnp.float32)))
    a = jax.nn.sigmoid(a0 + (xa @ a1) @ a2)

    def heads(t):
        return t.reshape(B, S, H, N).astype(jnp.float32)

    kk = heads(k * k_k)
    kk = kk / jnp.maximum(jnp.sqrt(jnp.sum(kk * kk, axis=-1, keepdims=True)), 1e-12)
    k = k * (1.0 + (a - 1.0) * k_a)
    r_h, k_h, v_h, a_h, w_h = heads(r), heads(k), heads(v), heads(a), heads(decay)

    def step(state, inp):
        r_t, w_t, k_t, v_t, kk_t, a_t = inp
        sa = jnp.einsum('bhvk,bhk->bhv', state, -kk_t)
        state = (state * w_t[:, :, None, :] + sa[..., None] * (kk_t * a_t)[:, :, None, :]
                 + v_t[..., None] * k_t[:, :, None, :])
        return state, jnp.einsum('bhvk,bhk->bhv', state, r_t)

    xs = tuple(t.transpose(1, 0, 2, 3) for t in (r_h, w_h, k_h, v_h, kk, a_h))
    s0 = jnp.zeros((B, H, N, N), jnp.float32)
    _, y = lax.scan(step, s0, xs)
    y = y.transpose(1, 0, 2, 3)
    mean = jnp.mean(y, axis=-1, keepdims=True)
    var = jnp.mean((y - mean) ** 2, axis=-1, keepdims=True)
    y = ((y - mean) * lax.rsqrt(var + LNX_EPS)).reshape(B, S, D) * lnx_g + lnx_b
    bonus = jnp.sum(r_h * k_h * r_k.reshape(H, N), axis=-1, keepdims=True) * v_h
    y = y + bonus.reshape(B, S, D)
    y = (y * jax.nn.silu(z.astype(jnp.float32))).astype(h.dtype)
    return y @ w_o


def setup_inputs(seed: int = 0) -> dict:
    key = jax.random.key(seed)
    keys = iter(jax.random.split(key, 64))
    D = D_MODEL
    E = (DEPTH + 1) // 2
    O = DEPTH // 2

    def nrm(shape, scale):
        return jax.random.normal(next(keys), shape, jnp.float32) * scale

    def uni(shape, lo, hi):
        return jax.random.uniform(next(keys), shape, jnp.float32, lo, hi)

    L, DK, HID = CMP_LEN, NSA_DK, NSA_CMP_HID
    return {
        "x": nrm((BATCH, SEQ, D), 1.0),
        "c": nrm((BATCH, D), 1.0),
        "norm_g": 1.0 + nrm((DEPTH, D), 0.02),
        "ada_w": nrm((DEPTH, D, 3 * D), D ** -0.5),
        "ada_b": nrm((DEPTH, 3 * D), 0.02),
        "final_g": 1.0 + nrm((D,), 0.02),
        "a_w_in": nrm((E, D, IN0_W), D ** -0.5),
        "a_w_out": nrm((E, MIX0_W, D), MIX0_W ** -0.5),
        "nsa_pe_k": nrm((E, L, DK), 0.5),
        "nsa_w1_k": nrm((E, L, DK, HID), (L * DK) ** -0.5),
        "nsa_b1_k": nrm((E, HID), 0.02),
        "nsa_w2_k": nrm((E, HID, DK), HID ** -0.5),
        "nsa_pe_v": nrm((E, L, DK), 0.5),
        "nsa_w1_v": nrm((E, L, DK, HID), (L * DK) ** -0.5),
        "nsa_b1_v": nrm((E, HID), 0.02),
        "nsa_w2_v": nrm((E, HID, DK), HID ** -0.5),
        "mla_qa_g": 1.0 + nrm((E, Q_LORA), 0.02),
        "mla_w_qb": nrm((E, Q_LORA, MLA_HEADS * (QK_NOPE + QK_ROPE)), Q_LORA ** -0.5),
        "mla_kva_g": 1.0 + nrm((E, KV_LORA), 0.02),
        "mla_w_kvb": nrm((E, KV_LORA, MLA_HEADS * (QK_NOPE + V_HEAD)), KV_LORA ** -0.5),
        "r_mu": uni((O, 6, D), 0.0, 1.0),
        "r_w_r": nrm((O, D, D), D ** -0.5),
        "r_w_k": nrm((O, D, D), D ** -0.5),
        "r_w_v": nrm((O, D, D), D ** -0.5),
        "r_w_z": nrm((O, D, D), D ** -0.5),
        "r_w_o": nrm((O, D, D), D ** -0.5),
        "r_w0": uni((O, D), -6.0, 1.0),
        "r_w1": nrm((O, D, DECAY_LORA), D ** -0.5),
        "r_w2": nrm((O, DECAY_LORA, D), 0.1 * DECAY_LORA ** -0.5),
        "r_a0": nrm((O, D), 0.1),
        "r_a1": nrm((O, D, A_LORA), D ** -0.5),
        "r_a2": nrm((O, A_LORA, D), 0.1 * A_LORA ** -0.5),
        "r_k_k": 0.85 + nrm((O, D), 0.05),
        "r_k_a": 1.0 + nrm((O, D), 0.05),
        "r_r_k": nrm((O, D), 0.1),
        "r_lnx_g": 1.0 + nrm((O, D), 0.02),
        "r_lnx_b": nrm((O, D), 0.02),
    }


def reference(x, c, norm_g, ada_w, ada_b, final_g,
              a_w_in, a_w_out, nsa_pe_k, nsa_w1_k, nsa_b1_k, nsa_w2_k,
              nsa_pe_v, nsa_w1_v, nsa_b1_v, nsa_w2_v,
              mla_qa_g, mla_w_qb, mla_kva_g, mla_w_kvb,
              r_mu, r_w_r, r_w_k, r_w_v, r_w_z, r_w_o, r_w0, r_w1, r_w2,
              r_a0, r_a1, r_a2, r_k_k, r_k_a, r_r_k, r_lnx_g, r_lnx_b):
    sc = jax.nn.silu(c)
    for i in range(DEPTH):
        mod = (sc @ ada_w[i] + ada_b[i])[:, None, :]
        shift, scale, gate = jnp.split(mod, 3, axis=-1)
        h = rmsnorm(x, norm_g[i]) * (1.0 + scale) + shift
        j = i // 2
        if i % 2 == 0:
            y = hybrid_attention_mixer(h, a_w_in[j], a_w_out[j],
                                       nsa_pe_k[j], nsa_w1_k[j], nsa_b1_k[j], nsa_w2_k[j],
                                       nsa_pe_v[j], nsa_w1_v[j], nsa_b1_v[j], nsa_w2_v[j],
                                       mla_qa_g[j], mla_w_qb[j], mla_kva_g[j], mla_w_kvb[j])
        else:
            y = rwkv7_mixer(h, r_mu[j], r_w_r[j], r_w_k[j], r_w_v[j], r_w_z[j], r_w_o[j],
                            r_w0[j], r_w1[j], r_w2[j], r_a0[j], r_a1[j], r_a2[j],
                            r_k_k[j], r_k_a[j], r_r_k[j], r_lnx_g[j], r_lnx_b[j])
        x = x + gate * y
    return rmsnorm(x, final_g)
```

```python
import functools
import math

import jax
import jax.numpy as jnp
from jax import lax
from jax.experimental import pallas as pl
from jax.experimental.pallas import tpu as pltpu

F32 = jnp.float32
BF16 = jnp.bfloat16
HI = lax.Precision.HIGHEST

EPS = 1e-6
NEG_INF = -1e30
FORCE = 1e9
NSA_HEADS = 16
NSA_GROUPS = 2
NSA_HPG = NSA_HEADS // NSA_GROUPS
NSA_DK = 64
CMP_LEN = 32
CMP_STRIDE = 16
SLC_LEN = 64
SLC_TOPK = 16
WINDOW = 512
MLA_HEADS = 8
Q_LORA = 512
KV_LORA = 512
QK_NOPE = 128
QK_ROPE = 64
V_HEAD = 128
ROPE_THETA = 10000.0
RWKV_HEAD = 64
LORA_PAD = 128
LNX_EPS = 64e-5

LANES = 128
VMEM_LIMIT_BYTES = 48 * 1024 * 1024

MM_TM = 1024
MM_TM_F32 = 512
MM_TN = 512
ROW_TM = 256
MLA_T = 512
NSA_TQ = 128
NSA_TK = 512
RWKV_CHUNK = 64
RWKV_HB = 8
SEL_LANES = 128

C_Q = 0
C_Z = 1024
C_MLAZ = 2048
C_QA = 3072
C_CKV = 3584
C_KV = 4096
C_KPE = 4864
C_G = 4992
IN0_COLS = 5120


def _cparams(*sem):
    return pltpu.CompilerParams(dimension_semantics=sem, vmem_limit_bytes=VMEM_LIMIT_BYTES)


def _dot(a, b, precision=None):
    return jnp.dot(a, b, preferred_element_type=F32, precision=precision)


def _dot_nt(a, b):
    return lax.dot_general(a, b, (((1,), (1,)), ((), ())), preferred_element_type=F32)


def _dot_tn(a, b):
    return lax.dot_general(a, b, (((0,), (0,)), ((), ())), preferred_element_type=F32)


def _silu(x):
    return x * jax.nn.sigmoid(x)


def _softplus(x):
    return jnp.maximum(x, 0.0) + jnp.log1p(jnp.exp(-jnp.abs(x)))


def _iota(shape, dim):
    return lax.broadcasted_iota(jnp.int32, shape, dim)


def _mod_kernel(c_ref, w_ref, b_ref, o_ref):
    c = c_ref[...]
    sc = jnp.broadcast_to(_silu(c), (8, c.shape[1]))
    o_ref[0] = _dot(sc, w_ref[0], HI) + b_ref[0]


def _modulation(c, ada_w, ada_b):
    depth, d, n = ada_w.shape
    tn = 768
    return pl.pallas_call(
        _mod_kernel,
        out_shape=jax.ShapeDtypeStruct((depth, 8, n), F32),
        grid=(depth, n // tn),
        in_specs=[
            pl.BlockSpec((1, d), lambda i, j: (0, 0)),
            pl.BlockSpec((1, d, tn), lambda i, j: (i, 0, j)),
            pl.BlockSpec((1, 1, tn), lambda i, j: (i, 0, j)),
        ],
        out_specs=pl.BlockSpec((1, 8, tn), lambda i, j: (i, 0, j)),
        compiler_params=_cparams("parallel", "parallel"),
        name="adaln_mod",
    )(c, ada_w, ada_b.reshape(depth, 1, n))


def _mm_kernel(*refs, prologue, epilogue):
    it = iter(refs)
    a_ref = next(it)
    b_ref = next(it)
    g_ref = next(it) if prologue != "none" else None
    sc_ref, sh_ref = (next(it), next(it)) if prologue == "rms_mod" else (None, None)
    res_ref, gate_ref = (next(it), next(it)) if epilogue == "residual" else (None, None)
    o_ref = next(it)
    if prologue == "none":
        a = a_ref[...]
    else:
        a_s = next(it)

        @pl.when(pl.program_id(1) == 0)
        def _():
            x = a_ref[...].astype(F32)
            y = x * lax.rsqrt(jnp.mean(x * x, axis=-1, keepdims=True) + EPS) * g_ref[...]
            if prologue == "rms_mod":
                y = y * (1.0 + sc_ref[0, 0:1, :]) + sh_ref[0, 0:1, :]
            a_s[...] = y.astype(BF16)

        a = a_s[...]
    acc = _dot(a, b_ref[...])
    if epilogue == "tanh":
        acc = jnp.tanh(acc)
    elif epilogue == "residual":
        acc = res_ref[...] + gate_ref[0, 0:1, :] * acc
    o_ref[...] = acc.astype(o_ref.dtype)


def _matmul(a, b, *, out_dtype, a_col=0, k=None, prologue="none", gain=None, mod=None, layer=0,
            epilogue="none", res=None, name="matmul"):
    m = a.shape[0]
    k = b.shape[0] if k is None else k
    n = b.shape[1]
    d_mod = None if mod is None else mod.shape[2] // 3
    tm = min(MM_TM if prologue == "none" else MM_TM_F32, m)
    tn = min(MM_TN, n)
    in_specs = [pl.BlockSpec((tm, k), lambda i, j: (i, a_col)), pl.BlockSpec((k, tn), lambda i, j: (0, j))]
    args = [a, b]
    scratch = []
    if prologue != "none":
        in_specs.append(pl.BlockSpec((1, k), lambda i, j: (0, 0)))
        args.append(gain.reshape(1, k).astype(F32))
        scratch.append(pltpu.VMEM((tm, k), BF16))
    if prologue == "rms_mod":
        in_specs += [pl.BlockSpec((1, 8, k), lambda i, j: (layer, 0, 1)), pl.BlockSpec((1, 8, k), lambda i, j: (layer, 0, 0))]
        args += [mod, mod]
    if epilogue == "residual":
        in_specs += [pl.BlockSpec((tm, tn), lambda i, j: (i, j)),
                     pl.BlockSpec((1, 8, tn), lambda i, j: (layer, 0, 2 * (d_mod // tn) + j))]
        args += [res, mod]
    return pl.pallas_call(
        functools.partial(_mm_kernel, prologue=prologue, epilogue=epilogue),
        out_shape=jax.ShapeDtypeStruct((m, n), out_dtype),
        grid=(m // tm, n // tn),
        in_specs=in_specs,
        out_specs=pl.BlockSpec((tm, tn), lambda i, j: (i, j)),
        scratch_shapes=scratch,
        compiler_params=_cparams("parallel", "arbitrary"),
        name=name,
    )(*args)


def _mla_attn_kernel(qn_ref, qp_ref, qpr_ref, cq_ref, sq_ref, k_ref, v_ref, kpe_ref, ck_ref, sk_ref, z_ref,
                     o_ref, qn_s, qp_s, m_s, l_s, acc_s, *, t, scale):
    qi = pl.program_id(0)
    ki = pl.program_id(1)

    @pl.when(ki == 0)
    def _():
        qn_s[...] = (qn_ref[...].astype(F32) * scale).astype(BF16)
        qp = qp_ref[...].astype(F32) * cq_ref[...] + qpr_ref[...].astype(F32) * sq_ref[...]
        qp_s[...] = (qp * scale).astype(BF16)
        m_s[...] = jnp.full(m_s.shape, NEG_INF, F32)
        l_s[...] = jnp.zeros(l_s.shape, F32)
        acc_s[...] = jnp.zeros(acc_s.shape, F32)

    @pl.when(ki <= qi)
    def _():
        kpe = kpe_ref[...]
        kp = (kpe[:, :QK_ROPE] * ck_ref[...] + kpe[:, QK_ROPE:] * sk_ref[...]).astype(BF16)
        visible = (qi * t + _iota((t, 1), 0)) >= (ki * t + _iota((1, t), 1))
        for h in range(MLA_HEADS):
            s = _dot_nt(qn_s[:, h * QK_NOPE:(h + 1) * QK_NOPE], k_ref[:, h * QK_NOPE:(h + 1) * QK_NOPE])
            s = s + _dot_nt(qp_s[:, h * QK_ROPE:(h + 1) * QK_ROPE], kp)
            s = jnp.where(visible, s, NEG_INF)
            m_prev = m_s[h]
            m_new = jnp.maximum(m_prev, jnp.max(s, axis=1, keepdims=True))
            p = jnp.where(visible, jnp.exp(s - m_new), 0.0)
            alpha = jnp.exp(m_prev - m_new)
            l_s[h] = alpha * l_s[h] + jnp.sum(p, axis=1, keepdims=True)
            m_s[h] = m_new
            acc_s[h] = alpha * acc_s[h] + _dot(p.astype(BF16), v_ref[:, h * V_HEAD:(h + 1) * V_HEAD])

    @pl.when(ki == qi)
    def _():
        for h in range(MLA_HEADS):
            sl = slice(h * V_HEAD, (h + 1) * V_HEAD)
            o_ref[:, sl] = (acc_s[h] / l_s[h] * _silu(z_ref[:, sl])).astype(o_ref.dtype)


def _mla_attention(qfull, kv, proj, cos_q, sin_q, cos_k, sin_k):
    s_len = qfull.shape[0]
    t = min(MLA_T, s_len)
    n = s_len // t
    hn = MLA_HEADS * QK_NOPE
    hp = MLA_HEADS * QK_ROPE
    hv = MLA_HEADS * V_HEAD
    kv_idx = lambda qi, ki: (jnp.minimum(ki, qi), 0)
    return pl.pallas_call(
        functools.partial(_mla_attn_kernel, t=t, scale=(QK_NOPE + QK_ROPE) ** -0.5),
        out_shape=jax.ShapeDtypeStruct((s_len, hv), BF16),
        grid=(n, n),
        in_specs=[
            pl.BlockSpec((t, hn), lambda qi, ki: (qi, 0)),
            pl.BlockSpec((t, hp), lambda qi, ki: (qi, hn // hp)),
            pl.BlockSpec((t, hp), lambda qi, ki: (qi, hn // hp + 1)),
            pl.BlockSpec((t, hp), lambda qi, ki: (qi, 0)),
            pl.BlockSpec((t, hp), lambda qi, ki: (qi, 0)),
            pl.BlockSpec((t, hn), kv_idx),
            pl.BlockSpec((t, hv), lambda qi, ki: (jnp.minimum(ki, qi), 1)),
            pl.BlockSpec((t, 2 * QK_ROPE), lambda qi, ki: (jnp.minimum(ki, qi), C_KPE // (2 * QK_ROPE))),
            pl.BlockSpec((t, QK_ROPE), kv_idx),
            pl.BlockSpec((t, QK_ROPE), kv_idx),
            pl.BlockSpec((t, hv), lambda qi, ki: (qi, C_MLAZ // hv)),
        ],
        out_specs=pl.BlockSpec((t, hv), lambda qi, ki: (qi, 0)),
        scratch_shapes=[
            pltpu.VMEM((t, hn), BF16),
            pltpu.VMEM((t, hp), BF16),
            pltpu.VMEM((MLA_HEADS, t, 1), F32),
            pltpu.VMEM((MLA_HEADS, t, 1), F32),
            pltpu.VMEM((MLA_HEADS, t, V_HEAD), F32),
        ],
        compiler_params=_cparams("parallel", "arbitrary"),
        name="mla_attention",
    )(qfull, qfull, qfull, cos_q, sin_q, kv, kv, proj, cos_k, sin_k, proj)


def _cmp_tokens_kernel(r_ref, w1_ref, pe_ref, b1_ref, w2_ref, o_ref, *, n_cmp):
    r = r_ref[0]
    w1 = w1_ref[0]
    half = w1.shape[0] // 2
    first = _dot(r, w1[:half], HI)
    second = _dot(r, w1[half:], HI)
    rows = r.shape[0]
    second = pltpu.roll(second, rows - 1, 0)
    hid = first + second + _dot(pe_ref[0], w1, HI) + b1_ref[0]
    out = _dot(_silu(hid), w2_ref[0], HI)
    o_ref[0] = jnp.where(_iota((rows, 1), 0) < n_cmp, out, 0.0)


def _compress_tokens(r, w1, pe, b1, w2, n_cmp):
    four, rows, width = r.shape
    hid = w1.shape[2]
    dk = w2.shape[2]
    return pl.pallas_call(
        functools.partial(_cmp_tokens_kernel, n_cmp=n_cmp),
        out_shape=jax.ShapeDtypeStruct((four, rows, dk), F32),
        grid=(four,),
        in_specs=[
            pl.BlockSpec((1, rows, width), lambda i: (i, 0, 0)),
            pl.BlockSpec((1, 2 * width, hid), lambda i: (i // NSA_GROUPS, 0, 0)),
            pl.BlockSpec((1, 1, 2 * width), lambda i: (i // NSA_GROUPS, 0, 0)),
            pl.BlockSpec((1, 1, hid), lambda i: (i // NSA_GROUPS, 0, 0)),
            pl.BlockSpec((1, hid, dk), lambda i: (i // NSA_GROUPS, 0, 0)),
        ],
        out_specs=pl.BlockSpec((1, rows, dk), lambda i: (i, 0, 0)),
        compiler_params=_cparams("parallel"),
        name="nsa_compress_tokens",
    )(r, w1, pe, b1, w2)


def _alibi_slope(h):
    return 2.0 ** (-8.0 * (h + 1) / NSA_HEADS)


def _nsa_cmp_kernel(q_ref, g_ref, cmp_ref, o_ref, sel_ref, *, tq, n_cmp, n_slc):
    qi = pl.program_id(0)
    n_pad = cmp_ref.shape[1]
    scale = NSA_DK ** -0.5
    t_col = qi * tq + _iota((tq, 1), 0)
    n_row = _iota((1, n_pad), 1)
    cmp_end = n_row * CMP_STRIDE + (CMP_LEN - 1)
    valid = (cmp_end <= t_col) & (n_row < n_cmp)
    dist = (t_col - cmp_end).astype(F32)
    gates = jax.nn.sigmoid(g_ref[...])
    q = q_ref[...]
    n_c = _iota((n_pad, SEL_LANES), 0) * CMP_STRIDE
    j_c = _iota((n_pad, SEL_LANES), 1) * SLC_LEN
    overlap = ((n_c < j_c + SLC_LEN) & (n_c + CMP_LEN > j_c)).astype(F32)
    j_row = _iota((1, SEL_LANES), 1)
    cur = lax.shift_right_logical(t_col, int(math.log2(SLC_LEN)))
    forced = (j_row == 0) | (j_row == cur) | (j_row == cur - 1)
    causal = (j_row * SLC_LEN <= t_col) & (j_row < n_slc)
    j_t = _iota((SEL_LANES, tq), 0)
    for g in range(NSA_GROUPS):
        kc = cmp_ref[g].astype(BF16)
        vc = cmp_ref[NSA_GROUPS + g].astype(BF16)
        psum = jnp.zeros((tq, n_pad), F32)
        for hh in range(NSA_HPG):
            h = g * NSA_HPG + hh
            sl = slice(h * NSA_DK, (h + 1) * NSA_DK)
            s = _dot_nt((q[:, sl] * scale).astype(BF16), kc) - _alibi_slope(h) * dist
            s = jnp.where(valid, s, NEG_INF)
            e = jnp.where(valid, jnp.exp(s - jnp.max(s, axis=1, keepdims=True)), 0.0)
            den = jnp.sum(e, axis=1, keepdims=True)
            p = e / jnp.where(den > 0.0, den, 1.0)
            psum = psum + p
            o_ref[:, sl] = _dot(p.astype(BF16), vc) * gates[:, 3 * h:3 * h + 1]
        imp = _dot(psum, overlap, HI)
        imp = jnp.where(forced, FORCE, imp)
        imp = jnp.where(causal, imp, jnp.where(j_row < n_slc, NEG_INF, -3e38))
        imp_t = imp.T
        sel_t = jnp.zeros((SEL_LANES, tq), F32)
        for _ in range(min(SLC_TOPK, n_slc)):
            best = jnp.max(imp_t, axis=0, keepdims=True)
            first = jnp.min(jnp.where(imp_t == best, j_t, SEL_LANES), axis=0, keepdims=True)
            hit = j_t == first
            sel_t = jnp.where(hit, 1.0, sel_t)
            imp_t = jnp.where(hit, -3.4e38, imp_t)
        sel_ref[:, g * SEL_LANES:(g + 1) * SEL_LANES] = sel_t.T.astype(sel_ref.dtype)


def _nsa_compressed(proj, cmp_tokens, n_cmp, n_slc):
    s_len = proj.shape[0]
    tq = NSA_TQ
    hd = NSA_HEADS * NSA_DK
    return pl.pallas_call(
        functools.partial(_nsa_cmp_kernel, tq=tq, n_cmp=n_cmp, n_slc=n_slc),
        out_shape=(jax.ShapeDtypeStruct((s_len, hd), F32),
                   jax.ShapeDtypeStruct((s_len, NSA_GROUPS * SEL_LANES), BF16)),
        grid=(s_len // tq,),
        in_specs=[
            pl.BlockSpec((tq, hd), lambda i: (i, C_Q // hd)),
            pl.BlockSpec((tq, LANES), lambda i: (i, C_G // LANES)),
            pl.BlockSpec(cmp_tokens.shape, lambda i: (0, 0, 0)),
        ],
        out_specs=(pl.BlockSpec((tq, hd), lambda i: (i, 0)),
                   pl.BlockSpec((tq, NSA_GROUPS * SEL_LANES), lambda i: (i, 0))),
        compiler_params=_cparams("parallel"),
        name="nsa_compressed_topk",
    )(proj, proj, cmp_tokens)


def _nsa_tile_range(mode, qi, tq, tk):
    q0 = qi * tq
    last = lax.div(q0 + tq - 1, tk)
    if mode == "slc":
        return jnp.zeros_like(last), last
    return lax.div(jnp.maximum(q0 - (WINDOW - 1), 0), tk), last


def _nsa_flash_kernel(*refs, mode, tq, tk, branch):
    if mode == "slc":
        (q_ref, g_ref, k_ref, v_ref, sel_ref, ocmp_ref, owin_ref, z_ref, o_ref,
         q_s, p_s, a_s, m_s, l_s, acc_s) = refs
    else:
        q_ref, g_ref, k_ref, v_ref, o_ref, q_s, p_s, a_s, m_s, l_s, acc_s = refs
    qi = pl.program_id(0)
    step = pl.program_id(1)
    lo, hi = _nsa_tile_range(mode, qi, tq, tk)
    kt = lo + step
    scale = NSA_DK ** -0.5

    @pl.when(step == 0)
    def _():
        for h in range(NSA_HEADS):
            g, hh = divmod(h, NSA_HPG)
            q_s[g, hh * tq:(hh + 1) * tq, :] = (q_ref[:, h * NSA_DK:(h + 1) * NSA_DK] * scale).astype(BF16)
        m_s[...] = jnp.full(m_s.shape, NEG_INF, F32)
        l_s[...] = jnp.zeros(l_s.shape, F32)
        acc_s[...] = jnp.zeros(acc_s.shape, F32)

    @pl.when(kt <= hi)
    def _():
        k0 = kt * tk
        dist = ((qi * tq + _iota((tq, 1), 0)) - (k0 + _iota((1, tk), 1))).astype(F32)
        if mode == "slc":
            blk = lax.div(k0, SLC_LEN) + lax.shift_right_logical(_iota((SEL_LANES, tk), 1), int(math.log2(SLC_LEN)))
            expand = (_iota((SEL_LANES, tk), 0) == blk).astype(BF16)
        else:
            allowed = (dist >= 0.0) & (dist < float(WINDOW))
        for g in range(NSA_GROUPS):
            if mode == "slc":
                picked = _dot(sel_ref[:, g * SEL_LANES:(g + 1) * SEL_LANES], expand)
                allowed = (picked > 0.5) & (dist >= 0.0)
            kg = k_ref[:, g * NSA_DK:(g + 1) * NSA_DK].astype(BF16)
            vg = v_ref[:, g * NSA_DK:(g + 1) * NSA_DK].astype(BF16)
            s_all = _dot_nt(q_s[g], kg)
            for hh in range(NSA_HPG):
                rows = slice(hh * tq, (hh + 1) * tq)
                s = s_all[rows] - _alibi_slope(g * NSA_HPG + hh) * dist
                s = jnp.where(allowed, s, NEG_INF)
                m_prev = m_s[g, rows]
                m_new = jnp.maximum(m_prev, jnp.max(s, axis=1, keepdims=True))
                p = jnp.where(allowed, jnp.exp(s - m_new), 0.0)
                alpha = jnp.exp(m_prev - m_new)
                l_s[g, rows] = alpha * l_s[g, rows] + jnp.sum(p, axis=1, keepdims=True)
                m_s[g, rows] = m_new
                a_s[rows] = alpha
                p_s[rows] = p.astype(BF16)
            acc_s[g] = a_s[...] * acc_s[g] + _dot(p_s[...], vg)

    @pl.when(step == pl.num_programs(1) - 1)
    def _():
        gates = jax.nn.sigmoid(g_ref[...])
        for h in range(NSA_HEADS):
            g, hh = divmod(h, NSA_HPG)
            rows = slice(hh * tq, (hh + 1) * tq)
            sl = slice(h * NSA_DK, (h + 1) * NSA_DK)
            o = acc_s[g, rows] / l_s[g, rows] * gates[:, 3 * h + branch:3 * h + branch + 1]
            if mode == "slc":
                o = (o + ocmp_ref[:, sl] + owin_ref[:, sl]) * _silu(z_ref[:, sl])
            o_ref[:, sl] = o.astype(o_ref.dtype)


def _nsa_flash(proj, mode, sel=None, o_cmp=None, o_win=None):
    s_len = proj.shape[0]
    tq, tk = NSA_TQ, min(NSA_TK, s_len)
    hd = NSA_HEADS * NSA_DK
    gd = NSA_GROUPS * NSA_DK
    k_slab, v_slab, branch = (2, 3, 1) if mode == "slc" else (4, 5, 2)
    nsteps = s_len // tk if mode == "slc" else -(-(WINDOW - 1) // tk) + 1

    def kv_map(slab):
        def index(qi, step):
            lo, hi = _nsa_tile_range(mode, qi, tq, tk)
            return jnp.minimum(lo + step, hi), C_KV // gd + slab
        return index

    q_map = lambda qi, step: (qi, 0)
    in_specs = [
        pl.BlockSpec((tq, hd), lambda qi, step: (qi, C_Q // hd)),
        pl.BlockSpec((tq, LANES), lambda qi, step: (qi, C_G // LANES)),
        pl.BlockSpec((tk, gd), kv_map(k_slab)),
        pl.BlockSpec((tk, gd), kv_map(v_slab)),
    ]
    args = [proj, proj, proj, proj]
    if mode == "slc":
        in_specs += [
            pl.BlockSpec((tq, NSA_GROUPS * SEL_LANES), q_map),
            pl.BlockSpec((tq, hd), q_map),
            pl.BlockSpec((tq, hd), q_map),
            pl.BlockSpec((tq, hd), lambda qi, step: (qi, C_Z // hd)),
        ]
        args += [sel, o_cmp, o_win, proj]
    rows = NSA_HPG * tq
    return pl.pallas_call(
        functools.partial(_nsa_flash_kernel, mode=mode, tq=tq, tk=tk, branch=branch),
        out_shape=jax.ShapeDtypeStruct((s_len, hd), BF16 if mode == "slc" else F32),
        grid=(s_len // tq, nsteps),
        in_specs=in_specs,
        out_specs=pl.BlockSpec((tq, hd), q_map),
        scratch_shapes=[
            pltpu.VMEM((NSA_GROUPS, rows, NSA_DK), BF16),
            pltpu.VMEM((rows, tk), BF16),
            pltpu.VMEM((rows, 1), F32),
            pltpu.VMEM((NSA_GROUPS, rows, 1), F32),
            pltpu.VMEM((NSA_GROUPS, rows, 1), F32),
            pltpu.VMEM((NSA_GROUPS, rows, NSA_DK), F32),
        ],
        compiler_params=_cparams("parallel", "arbitrary"),
        name="nsa_selected" if mode == "slc" else "nsa_window",
    )(*args)


def _shift_lerp_kernel(x_ref, xp_ref, g_ref, sc_ref, sh_ref, mu_ref, *o_refs, tm):
    def modulated(x):
        y = x * lax.rsqrt(jnp.mean(x * x, axis=-1, keepdims=True) + EPS) * g_ref[...]
        return y * (1.0 + sc_ref[0, 0:1, :]) + sh_ref[0, 0:1, :]

    h = modulated(x_ref[...])
    before = modulated(xp_ref[...])[7:8, :] * jnp.where(pl.program_id(0) > 0, 1.0, 0.0)
    prev = jnp.where(_iota((tm, 1), 0) == 0, before, pltpu.roll(h, 1, 0))
    xx = prev - h
    for j, o_ref in enumerate(o_refs):
        o_ref[...] = (h + xx * mu_ref[j:j + 1, :]).astype(o_ref.dtype)


def _shift_lerp(x, gain, mod, layer, mu):
    s_len, d = x.shape
    tm = min(ROW_TM, s_len)
    row = pl.BlockSpec((tm, d), lambda i: (i, 0))
    return pl.pallas_call(
        functools.partial(_shift_lerp_kernel, tm=tm),
        out_shape=tuple(jax.ShapeDtypeStruct((s_len, d), BF16) for _ in range(6)),
        grid=(s_len // tm,),
        in_specs=[
            row,
            pl.BlockSpec((8, d), lambda i: (jnp.maximum(i * (tm // 8) - 1, 0), 0)),
            pl.BlockSpec((1, d), lambda i: (0, 0)),
            pl.BlockSpec((1, 8, d), lambda i: (layer, 0, 1)),
            pl.BlockSpec((1, 8, d), lambda i: (layer, 0, 0)),
            pl.BlockSpec((6, d), lambda i: (0, 0)),
        ],
        out_specs=tuple(row for _ in range(6)),
        compiler_params=_cparams("parallel"),
        name="rwkv_shift_lerp",
    )(x, x, gain.reshape(1, d), mod, mod, mu)


def _head_block_diag(value):
    shift = int(math.log2(RWKV_HEAD))
    same = (lax.shift_right_logical(_iota((LANES, LANES), 0), shift)
            == lax.shift_right_logical(_iota((LANES, LANES), 1), shift))
    return jnp.where(same, value, 0.0).astype(F32)


def _rwkv_prep_kernel(k_ref, wl_ref, al_ref, w0_ref, a0_ref, kk_ref, ka_ref, lw_o, kk_o, kka_o, k2_o):
    k = k_ref[...]
    lw_o[...] = -jnp.exp(-_softplus(-(w0_ref[...] + wl_ref[...])) - 0.5)
    a = jax.nn.sigmoid(a0_ref[...] + al_ref[...])
    kkr = k * kk_ref[...]
    ones = _head_block_diag(1.0)
    for j in range(k.shape[1] // LANES):
        sl = slice(j * LANES, (j + 1) * LANES)
        x = kkr[:, sl]
        kk = x / jnp.maximum(jnp.sqrt(_dot(x * x, ones, HI)), 1e-12)
        kk_o[:, sl] = kk
        kka_o[:, sl] = kk * a[:, sl]
    k2_o[...] = k * (1.0 + (a - 1.0) * ka_ref[...])


def _rwkv_prep(k, wl, al, w0, a0, k_k, k_a):
    s_len, d = k.shape
    tm = min(ROW_TM, s_len)
    row = pl.BlockSpec((tm, d), lambda i: (i, 0))
    vec = pl.BlockSpec((1, d), lambda i: (0, 0))
    return pl.pallas_call(
        _rwkv_prep_kernel,
        out_shape=tuple(jax.ShapeDtypeStruct((s_len, d), F32) for _ in range(4)),
        grid=(s_len // tm,),
        in_specs=[row, row, row, vec, vec, vec, vec],
        out_specs=(row, row, row, row),
        compiler_params=_cparams("parallel"),
        name="rwkv_prep",
    )(k, wl, al, w0.reshape(1, d), a0.reshape(1, d), k_k.reshape(1, d), k_a.reshape(1, d))


def _rwkv_chunk_kernel(r_ref, lw_ref, k_ref, v_ref, kk_ref, kka_ref, y_ref, s_s, *, c, hb):
    n = RWKV_HEAD

    @pl.when(pl.program_id(1) == 0)
    def _():
        s_s[...] = jnp.zeros(s_s.shape, F32)

    rr = _iota((c, c), 0)
    cc = _iota((c, c), 1)
    lw = lw_ref[...]
    cum = _dot((rr >= cc).astype(F32), lw, HI)
    grow = jnp.exp(-cum)
    decay = jnp.exp(cum)
    total = decay[c - 1:c, :]
    kk = kk_ref[...]
    q_t = kk * jnp.exp(cum - lw)
    p_t = -(kka_ref[...] * grow)
    k_t = k_ref[...] * grow
    r_t = r_ref[...] * decay
    p_end = p_t * total
    k_end = k_t * total
    r2 = _iota((2 * c, 2 * c), 0)
    c2 = _iota((2 * c, 2 * c), 1) & (c - 1)
    gram_mask = (((r2 < c) & (r2 > c2)) | ((r2 >= c) & ((r2 - c) >= c2))).astype(F32)
    eye = (rr == cc).astype(F32)
    same_block = [(lax.shift_right_logical(rr, s) == lax.shift_right_logical(cc, s)).astype(F32)
                  for s in range(3, int(math.log2(c)) + 1)]
    for i in range(hb):
        sl = slice(i * n, (i + 1) * n)
        qr = jnp.concatenate([q_t[:, sl], r_t[:, sl]], axis=0).astype(BF16)
        pk = jnp.concatenate([p_t[:, sl], k_t[:, sl]], axis=0).astype(BF16)
        gram = _dot_nt(qr, pk) * gram_mask
        l_qp = gram[:c, :c]
        power = l_qp * same_block[0]
        inv = eye + power
        for _ in range(2):
            pb = power.astype(BF16)
            power = _dot(pb, pb)
            inv = inv + _dot(inv.astype(BF16), power.astype(BF16))
        for lvl in range(1, len(same_block)):
            below = (l_qp * (same_block[lvl] - same_block[lvl - 1])).astype(BF16)
            ib = inv.astype(BF16)
            inv = inv + _dot(_dot(ib, below).astype(BF16), ib)
        v = v_ref[:, sl]
        from_v = _dot(jnp.concatenate([gram[:c, c:], gram[c:, c:]], axis=0).astype(BF16), v.astype(BF16))
        state = s_s[i]
        from_state = _dot_nt(qr, state.astype(BF16))
        u = _dot(inv.astype(BF16), (from_state[:c] + from_v[:c]).astype(BF16))
        y_ref[:, sl] = from_state[c:] + _dot(gram[c:, :c].astype(BF16), u.astype(BF16)) + from_v[c:]
        uv = jnp.concatenate([u, v], axis=0).astype(BF16)
        pk_end = jnp.concatenate([p_end[:, sl], k_end[:, sl]], axis=0).astype(BF16)
        s_s[i] = state * total[:, sl] + _dot_tn(uv, pk_end)


def _rwkv_scan(r, lw, k2, v, kk, kka):
    s_len, d = r.shape
    c, hb = RWKV_CHUNK, RWKV_HB
    width = hb * RWKV_HEAD
    blk = pl.BlockSpec((c, width), lambda hg, ci: (ci, hg))
    return pl.pallas_call(
        functools.partial(_rwkv_chunk_kernel, c=c, hb=hb),
        out_shape=jax.ShapeDtypeStruct((s_len, d), F32),
        grid=(d // width, s_len // c),
        in_specs=[blk] * 6,
        out_specs=blk,
        scratch_shapes=[pltpu.VMEM((hb, RWKV_HEAD, RWKV_HEAD), F32)],
        compiler_params=_cparams("parallel", "arbitrary"),
        name="rwkv_chunk_scan",
    )(r, lw, k2, v, kk, kka)


def _rwkv_post_kernel(y_ref, r_ref, k2_ref, v_ref, z_ref, rk_ref, g_ref, b_ref, o_ref):
    avg = _head_block_diag(1.0 / RWKV_HEAD)
    ones = _head_block_diag(1.0)
    for j in range(y_ref.shape[1] // LANES):
        sl = slice(j * LANES, (j + 1) * LANES)
        y = y_ref[:, sl]
        dev = y - _dot(y, avg, HI)
        var = _dot(dev * dev, avg, HI)
        yn = dev * lax.rsqrt(var + LNX_EPS) * g_ref[:, sl] + b_ref[:, sl]
        bonus = _dot(r_ref[:, sl] * k2_ref[:, sl] * rk_ref[:, sl], ones, HI) * v_ref[:, sl]
        o_ref[:, sl] = ((yn + bonus) * _silu(z_ref[:, sl])).astype(o_ref.dtype)


def _rwkv_post(y, r, k2, v, z, r_k, lnx_g, lnx_b):
    s_len, d = y.shape
    tm = min(ROW_TM, s_len)
    row = pl.BlockSpec((tm, d), lambda i: (i, 0))
    vec = pl.BlockSpec((1, d), lambda i: (0, 0))
    return pl.pallas_call(
        _rwkv_post_kernel,
        out_shape=jax.ShapeDtypeStruct((s_len, d), BF16),
        grid=(s_len // tm,),
        in_specs=[row, row, row, row, row, vec, vec, vec],
        out_specs=row,
        compiler_params=_cparams("parallel"),
        name="rwkv_post",
    )(y, r, k2, v, z, r_k.reshape(1, d), lnx_g.reshape(1, d), lnx_b.reshape(1, d))


def _rmsnorm_kernel(x_ref, g_ref, o_ref):
    x = x_ref[...].astype(F32)
    o_ref[...] = (x * lax.rsqrt(jnp.mean(x * x, axis=-1, keepdims=True) + EPS) * g_ref[...]).astype(o_ref.dtype)


def _rmsnorm(x, gain, out_dtype):
    s_len, d = x.shape
    tm = min(ROW_TM, s_len)
    return pl.pallas_call(
        _rmsnorm_kernel,
        out_shape=jax.ShapeDtypeStruct((s_len, d), out_dtype),
        grid=(s_len // tm,),
        in_specs=[pl.BlockSpec((tm, d), lambda i: (i, 0)), pl.BlockSpec((1, d), lambda i: (0, 0))],
        out_specs=pl.BlockSpec((tm, d), lambda i: (i, 0)),
        compiler_params=_cparams("parallel"),
        name="rmsnorm",
    )(x, gain.reshape(1, d))


def _rotate_half_cols(w):
    half = w.shape[1] // 2
    return jnp.concatenate([-w[:, half:], w[:, :half]], axis=1)


def _arrange_w_in(w_in):
    d = w_in.shape[0]
    q, kv, g, z, qa, kva, mlaz = jnp.split(w_in, [1024, 1792, 1840, 2864, 3376, 3952], axis=1)
    ckv, kpe = kva[:, :KV_LORA], kva[:, KV_LORA:]
    pad = jnp.zeros((d, IN0_COLS - C_G - g.shape[1]), w_in.dtype)
    return jnp.concatenate([q, z, mlaz, qa, ckv, kv, kpe, _rotate_half_cols(kpe), g, pad], axis=1).astype(BF16)


def _arrange_w_qb(w_qb):
    w = w_qb.reshape(Q_LORA, MLA_HEADS, QK_NOPE + QK_ROPE)
    nope = w[:, :, :QK_NOPE].reshape(Q_LORA, -1)
    pe = w[:, :, QK_NOPE:]
    half = QK_ROPE // 2
    rot = jnp.concatenate([-pe[:, :, half:], pe[:, :, :half]], axis=2)
    return jnp.concatenate([nope, pe.reshape(Q_LORA, -1), rot.reshape(Q_LORA, -1)], axis=1).astype(BF16)


def _arrange_w_kvb(w_kvb):
    w = w_kvb.reshape(KV_LORA, MLA_HEADS, QK_NOPE + V_HEAD)
    return jnp.concatenate([w[:, :, :QK_NOPE].reshape(KV_LORA, -1), w[:, :, QK_NOPE:].reshape(KV_LORA, -1)],
                           axis=1).astype(BF16)


def _rope_tables(s_len):
    inv = ROPE_THETA ** (-jnp.arange(0, QK_ROPE, 2, dtype=F32) / QK_ROPE)
    ang = jnp.arange(s_len, dtype=F32)[:, None] * inv[None]
    cos = jnp.concatenate([jnp.cos(ang), jnp.cos(ang)], axis=1)
    sin = jnp.concatenate([jnp.sin(ang), jnp.sin(ang)], axis=1)
    return cos, sin


def _pad_cols(w, n):
    return jnp.pad(w, ((0, 0), (0, n - w.shape[1])))


def _pad_rows(w, n):
    return jnp.pad(w, ((0, n - w.shape[0]), (0, 0)))


def _attention_layer(x, mod, layer, gain, w_in, w_out, pe_k, w1_k, b1_k, w2_k, pe_v, w1_v, b1_v, w2_v,
                     qa_g, w_qb, kva_g, w_kvb):
    s_len, d = x.shape
    assert s_len % 1024 == 0 and s_len // SLC_LEN <= SEL_LANES
    n_cmp = (s_len - CMP_LEN) // CMP_STRIDE + 1
    n_slc = s_len // SLC_LEN
    proj = _matmul(x, _arrange_w_in(w_in), out_dtype=F32, prologue="rms_mod", gain=gain, mod=mod, layer=layer,
                   name="in_proj")
    qfull = _matmul(proj, _arrange_w_qb(w_qb), out_dtype=BF16, a_col=C_QA // Q_LORA, k=Q_LORA, prologue="rms",
                    gain=qa_g, name="mla_q_proj")
    kv = _matmul(proj, _arrange_w_kvb(w_kvb), out_dtype=BF16, a_col=C_CKV // KV_LORA, k=KV_LORA, prologue="rms",
                 gain=kva_g, name="mla_kv_proj")
    cos, sin = _rope_tables(s_len)
    y_mla = _mla_attention(qfull, kv, proj, jnp.tile(cos, (1, MLA_HEADS)), jnp.tile(sin, (1, MLA_HEADS)), cos, sin)
    cl = CMP_LEN * NSA_DK
    kcvc = proj[:, C_KV:C_KV + 2 * NSA_GROUPS * NSA_DK].reshape(s_len, 2, NSA_GROUPS, NSA_DK)
    r = kcvc.transpose(1, 2, 0, 3).reshape(2 * NSA_GROUPS, s_len // CMP_STRIDE, CMP_STRIDE * NSA_DK)
    cmp_tokens = _compress_tokens(
        r,
        jnp.stack([w1_k.reshape(cl, -1), w1_v.reshape(cl, -1)]),
        jnp.stack([pe_k.reshape(1, cl), pe_v.reshape(1, cl)]),
        jnp.stack([b1_k.reshape(1, -1), b1_v.reshape(1, -1)]),
        jnp.stack([w2_k, w2_v]),
        n_cmp)
    o_cmp, sel = _nsa_compressed(proj, cmp_tokens, n_cmp, n_slc)
    o_win = _nsa_flash(proj, "win")
    y_nsa = _nsa_flash(proj, "slc", sel=sel, o_cmp=o_cmp, o_win=o_win)
    y = jnp.concatenate([y_nsa, y_mla], axis=1)
    return _matmul(y, w_out.astype(BF16), out_dtype=F32, epilogue="residual", res=x, mod=mod, layer=layer,
                   name="attn_out_proj")


def _rwkv_layer(x, mod, layer, gain, mu, w_r, w_k, w_v, w_z, w_o, w0, w1, w2, a0, a1, a2, k_k, k_a, r_k,
                lnx_g, lnx_b):
    xr, xw, xk, xv, xa, xz = _shift_lerp(x, gain, mod, layer, mu)
    r = _matmul(xr, w_r.astype(BF16), out_dtype=F32, name="rwkv_r")
    k = _matmul(xk, w_k.astype(BF16), out_dtype=F32, name="rwkv_k")
    v = _matmul(xv, w_v.astype(BF16), out_dtype=F32, name="rwkv_v")
    z = _matmul(xz, w_z.astype(BF16), out_dtype=F32, name="rwkv_z")
    wl = _matmul(xw, _pad_cols(w1, LORA_PAD).astype(BF16), out_dtype=BF16, epilogue="tanh", name="rwkv_w_lora_a")
    wl = _matmul(wl, _pad_rows(w2, LORA_PAD).astype(BF16), out_dtype=F32, name="rwkv_w_lora_b")
    al = _matmul(xa, _pad_cols(a1, LORA_PAD).astype(BF16), out_dtype=BF16, name="rwkv_a_lora_a")
    al = _matmul(al, _pad_rows(a2, LORA_PAD).astype(BF16), out_dtype=F32, name="rwkv_a_lora_b")
    lw, kk, kka, k2 = _rwkv_prep(k, wl, al, w0, a0, k_k, k_a)
    y = _rwkv_scan(r, lw, k2, v, kk, kka)
    y = _rwkv_post(y, r, k2, v, z, r_k, lnx_g, lnx_b)
    return _matmul(y, w_o.astype(BF16), out_dtype=F32, epilogue="residual", res=x, mod=mod, layer=layer,
                   name="rwkv_out_proj")


def kernel(x, c, norm_g, ada_w, ada_b, final_g, a_w_in, a_w_out, nsa_pe_k, nsa_w1_k, nsa_b1_k, nsa_w2_k, nsa_pe_v, nsa_w1_v, nsa_b1_v, nsa_w2_v, mla_qa_g, mla_w_qb, mla_kva_g, mla_w_kvb, r_mu, r_w_r, r_w_k, r_w_v, r_w_z, r_w_o, r_w0, r_w1, r_w2, r_a0, r_a1, r_a2, r_k_k, r_k_a, r_r_k, r_lnx_g, r_lnx_b):
    batch, s_len, d = x.shape
    depth = ada_w.shape[0]
    outs = []
    for b in range(batch):
        mod = _modulation(c[b:b + 1], ada_w, ada_b)
        xb = x[b]
        for i in range(depth):
            j = i // 2
            if i % 2 == 0:
                xb = _attention_layer(xb, mod, i, norm_g[i], a_w_in[j], a_w_out[j],
                                      nsa_pe_k[j], nsa_w1_k[j], nsa_b1_k[j], nsa_w2_k[j],
                                      nsa_pe_v[j], nsa_w1_v[j], nsa_b1_v[j], nsa_w2_v[j],
                                      mla_qa_g[j], mla_w_qb[j], mla_kva_g[j], mla_w_kvb[j])
            else:
                xb = _rwkv_layer(xb, mod, i, norm_g[i], r_mu[j], r_w_r[j], r_w_k[j], r_w_v[j], r_w_z[j], r_w_o[j],
                                 r_w0[j], r_w1[j], r_w2[j], r_a0[j], r_a1[j], r_a2[j], r_k_k[j], r_k_a[j],
                                 r_r_k[j], r_lnx_g[j], r_lnx_b[j])
        outs.append(_rmsnorm(xb, final_g, x.dtype))
    return jnp.stack(outs)
```

```python
import functools
import math

import jax
import jax.numpy as jnp
import numpy as np
from jax import lax
from jax.experimental import pallas as pl
from jax.experimental.pallas import tpu as pltpu

F32 = jnp.float32
BF16 = jnp.bfloat16
HI = lax.Precision.HIGHEST

EPS = 1e-6
NEG_INF = -1e30
FORCE = 1e9
NSA_HEADS = 16
NSA_GROUPS = 2
NSA_HPG = NSA_HEADS // NSA_GROUPS
NSA_DK = 64
CMP_LEN = 32
CMP_STRIDE = 16
SLC_LEN = 64
SLC_TOPK = 16
WINDOW = 512
MLA_HEADS = 8
Q_LORA = 512
KV_LORA = 512
QK_NOPE = 128
QK_ROPE = 64
V_HEAD = 128
ROPE_THETA = 10000.0
RWKV_HEAD = 64
LORA_PAD = 128
LNX_EPS = 64e-5

LANES = 128
VMEM_LIMIT_BYTES = 48 * 1024 * 1024

MM_TM = 1024
MM_TM_F32 = 512
MM_TN = 512
ROW_TM = 256
MLA_T = 512
NSA_TQ = 128
NSA_TK = 512
RWKV_CHUNK = 64
RWKV_HB = 32
SEL_LANES = 128
SM_ROWS = 64

LOG2E = 1.0 / math.log(2.0)
M_FLOOR = -1e20

C_Q = 0
C_Z = 1024
C_MLAZ = 2048
C_QA = 3072
C_CKV = 3584
C_KV = 4096
C_KPE = 4864
C_G = 4992
IN0_COLS = 5120


def _cparams(*sem):
    return pltpu.CompilerParams(dimension_semantics=sem, vmem_limit_bytes=VMEM_LIMIT_BYTES)


def _dot(a, b, precision=None):
    return jnp.dot(a, b, preferred_element_type=F32, precision=precision)


def _dot_nt(a, b):
    return lax.dot_general(a, b, (((1,), (1,)), ((), ())), preferred_element_type=F32)


def _dot_tn(a, b):
    return lax.dot_general(a, b, (((0,), (0,)), ((), ())), preferred_element_type=F32)


def _silu(x):
    return x * jax.nn.sigmoid(x)


def _softplus(x):
    return jnp.maximum(x, 0.0) + jnp.log1p(jnp.exp(-jnp.abs(x)))


def _iota(shape, dim):
    return lax.broadcasted_iota(jnp.int32, shape, dim)


def _mod_kernel(c_ref, w_ref, b_ref, o_ref):
    c = c_ref[...]
    sc = jnp.broadcast_to(_silu(c), (8, c.shape[1]))
    o_ref[0] = _dot(sc, w_ref[0], HI) + b_ref[0]


def _modulation(c, ada_w, ada_b):
    depth, d, n = ada_w.shape
    tn = 768
    return pl.pallas_call(
        _mod_kernel,
        out_shape=jax.ShapeDtypeStruct((depth, 8, n), F32),
        grid=(depth, n // tn),
        in_specs=[
            pl.BlockSpec((1, d), lambda i, j: (0, 0)),
            pl.BlockSpec((1, d, tn), lambda i, j: (i, 0, j)),
            pl.BlockSpec((1, 1, tn), lambda i, j: (i, 0, j)),
        ],
        out_specs=pl.BlockSpec((1, 8, tn), lambda i, j: (i, 0, j)),
        compiler_params=_cparams("parallel", "parallel"),
        name="adaln_mod",
    )(c, ada_w, ada_b.reshape(depth, 1, n))


def _mm_kernel(*refs, prologue, epilogue):
    it = iter(refs)
    a_ref = next(it)
    b_ref = next(it)
    g_ref = next(it) if prologue != "none" else None
    sc_ref, sh_ref = (next(it), next(it)) if prologue == "rms_mod" else (None, None)
    res_ref, gate_ref = (next(it), next(it)) if epilogue == "residual" else (None, None)
    o_ref = next(it)
    if prologue == "none":
        a = a_ref[...]
    else:
        a_s = next(it)

        @pl.when(pl.program_id(1) == 0)
        def _():
            x = a_ref[...].astype(F32)
            y = x * lax.rsqrt(jnp.mean(x * x, axis=-1, keepdims=True) + EPS) * g_ref[...]
            if prologue == "rms_mod":
                y = y * (1.0 + sc_ref[0, 0:1, :]) + sh_ref[0, 0:1, :]
            a_s[...] = y.astype(BF16)

        a = a_s[...]
    acc = _dot(a, b_ref[...])
    if epilogue == "tanh":
        acc = jnp.tanh(acc)
    elif epilogue == "residual":
        acc = res_ref[...] + gate_ref[0, 0:1, :] * acc
    o_ref[...] = acc.astype(o_ref.dtype)


def _matmul(a, b, *, out_dtype, a_col=0, k=None, prologue="none", gain=None, mod=None, layer=0,
            epilogue="none", res=None, name="matmul"):
    m = a.shape[0]
    k = b.shape[0] if k is None else k
    n = b.shape[1]
    d_mod = None if mod is None else mod.shape[2] // 3
    tm = min(MM_TM if prologue == "none" else MM_TM_F32, m)
    tn = min(MM_TN, n)
    in_specs = [pl.BlockSpec((tm, k), lambda i, j: (i, a_col)), pl.BlockSpec((k, tn), lambda i, j: (0, j))]
    args = [a, b]
    scratch = []
    if prologue != "none":
        in_specs.append(pl.BlockSpec((1, k), lambda i, j: (0, 0)))
        args.append(gain.reshape(1, k).astype(F32))
        scratch.append(pltpu.VMEM((tm, k), BF16))
    if prologue == "rms_mod":
        in_specs += [pl.BlockSpec((1, 8, k), lambda i, j: (layer, 0, 1)), pl.BlockSpec((1, 8, k), lambda i, j: (layer, 0, 0))]
        args += [mod, mod]
    if epilogue == "residual":
        in_specs += [pl.BlockSpec((tm, tn), lambda i, j: (i, j)),
                     pl.BlockSpec((1, 8, tn), lambda i, j: (layer, 0, 2 * (d_mod // tn) + j))]
        args += [res, mod]
    return pl.pallas_call(
        functools.partial(_mm_kernel, prologue=prologue, epilogue=epilogue),
        out_shape=jax.ShapeDtypeStruct((m, n), out_dtype),
        grid=(m // tm, n // tn),
        in_specs=in_specs,
        out_specs=pl.BlockSpec((tm, tn), lambda i, j: (i, j)),
        scratch_shapes=scratch,
        compiler_params=_cparams("parallel", "arbitrary"),
        name=name,
    )(*args)


def _mla_attn_kernel(qn_ref, qp_ref, qpr_ref, cq_ref, sq_ref, k_ref, v_ref, kpe_ref, ck_ref, sk_ref, z_ref,
                     o_ref, q_s, kc_s, s_s, p_s, a_s, m_s, l_s, acc_s, *, t, scale):
    qi = pl.program_id(0)
    ki = pl.program_id(1)
    dq = QK_NOPE + QK_ROPE

    @pl.when(ki == 0)
    def _():
        qn = (qn_ref[...].astype(F32) * (scale * LOG2E)).astype(BF16)
        qp = qp_ref[...].astype(F32) * cq_ref[...] + qpr_ref[...].astype(F32) * sq_ref[...]
        qp = (qp * (scale * LOG2E)).astype(BF16)
        for h in range(MLA_HEADS):
            q_s[:, h * dq:h * dq + QK_NOPE] = qn[:, h * QK_NOPE:(h + 1) * QK_NOPE]
            q_s[:, h * dq + QK_NOPE:(h + 1) * dq] = qp[:, h * QK_ROPE:(h + 1) * QK_ROPE]
        m_s[...] = jnp.full(m_s.shape, M_FLOOR, F32)
        l_s[...] = jnp.zeros(l_s.shape, F32)
        acc_s[...] = jnp.zeros(acc_s.shape, F32)

    def lane_tile_reduce(x, op):
        parts = [x[:, i * LANES:(i + 1) * LANES] for i in range(x.shape[1] // LANES)]
        while len(parts) > 1:
            parts = [op(parts[i], parts[i + 1]) for i in range(0, len(parts), 2)]
        return parts[0]

    def step(diagonal):
        kpe = kpe_ref[...]
        kp = (kpe[:, :QK_ROPE] * ck_ref[...] + kpe[:, QK_ROPE:] * sk_ref[...]).astype(BF16)
        for h in range(MLA_HEADS):
            kc_s[h, :, 0:QK_NOPE] = k_ref[:, h * QK_NOPE:(h + 1) * QK_NOPE]
            kc_s[h, :, QK_NOPE:dq] = kp
        for h in range(MLA_HEADS):
            s = _dot_nt(q_s[:, h * dq:(h + 1) * dq], kc_s[h])
            if diagonal:
                s = jnp.where(_iota((t, 1), 0) >= _iota((1, t), 1), s, NEG_INF)
            s_s[h] = s
            row_max = jnp.max(lane_tile_reduce(s, jnp.maximum), axis=1, keepdims=True)
            m_prev = m_s[h]
            m_new = jnp.maximum(m_prev, jnp.broadcast_to(row_max, (t, LANES)))
            a_s[h] = jnp.exp2(m_prev - m_new)
            m_s[h] = m_new
        for h in range(MLA_HEADS):
            for rb in range(t // SM_ROWS):
                r = slice(rb * SM_ROWS, (rb + 1) * SM_ROWS)
                p = jnp.exp2(s_s[h, r] - jnp.tile(m_s[h, r], (1, t // LANES)))
                p_s[h, r] = p.astype(BF16)
                row_sum = jnp.sum(lane_tile_reduce(p, jnp.add), axis=1, keepdims=True)
                l_s[h, r] = a_s[h, r] * l_s[h, r] + jnp.broadcast_to(row_sum, (SM_ROWS, LANES))
        for h in range(MLA_HEADS):
            acc_s[h] = a_s[h] * acc_s[h] + _dot(p_s[h], v_ref[:, h * V_HEAD:(h + 1) * V_HEAD])

    @pl.when(ki < qi)
    def _():
        step(False)

    @pl.when(ki == qi)
    def _():
        step(True)
        for h in range(MLA_HEADS):
            sl = slice(h * V_HEAD, (h + 1) * V_HEAD)
            o_ref[:, sl] = (acc_s[h] / l_s[h] * _silu(z_ref[:, sl])).astype(o_ref.dtype)


def _mla_attention(qfull, kv, proj, cos_q, sin_q, cos_k, sin_k):
    s_len = qfull.shape[0]
    t = min(MLA_T, s_len)
    n = s_len // t
    hn = MLA_HEADS * QK_NOPE
    hp = MLA_HEADS * QK_ROPE
    hv = MLA_HEADS * V_HEAD
    kv_idx = lambda qi, ki: (jnp.minimum(ki, qi), 0)
    return pl.pallas_call(
        functools.partial(_mla_attn_kernel, t=t, scale=(QK_NOPE + QK_ROPE) ** -0.5),
        out_shape=jax.ShapeDtypeStruct((s_len, hv), BF16),
        grid=(n, n),
        in_specs=[
            pl.BlockSpec((t, hn), lambda qi, ki: (qi, 0)),
            pl.BlockSpec((t, hp), lambda qi, ki: (qi, hn // hp)),
            pl.BlockSpec((t, hp), lambda qi, ki: (qi, hn // hp + 1)),
            pl.BlockSpec((t, hp), lambda qi, ki: (qi, 0)),
            pl.BlockSpec((t, hp), lambda qi, ki: (qi, 0)),
            pl.BlockSpec((t, hn), kv_idx),
            pl.BlockSpec((t, hv), lambda qi, ki: (jnp.minimum(ki, qi), 1)),
            pl.BlockSpec((t, 2 * QK_ROPE), lambda qi, ki: (jnp.minimum(ki, qi), C_KPE // (2 * QK_ROPE))),
            pl.BlockSpec((t, QK_ROPE), kv_idx),
            pl.BlockSpec((t, QK_ROPE), kv_idx),
            pl.BlockSpec((t, hv), lambda qi, ki: (qi, C_MLAZ // hv)),
        ],
        out_specs=pl.BlockSpec((t, hv), lambda qi, ki: (qi, 0)),
        scratch_shapes=[
            pltpu.VMEM((t, hn + hp), BF16),
            pltpu.VMEM((MLA_HEADS, t, QK_NOPE + QK_ROPE), BF16),
            pltpu.VMEM((MLA_HEADS, t, t), F32),
            pltpu.VMEM((MLA_HEADS, t, t), BF16),
            pltpu.VMEM((MLA_HEADS, t, LANES), F32),
            pltpu.VMEM((MLA_HEADS, t, LANES), F32),
            pltpu.VMEM((MLA_HEADS, t, LANES), F32),
            pltpu.VMEM((MLA_HEADS, t, V_HEAD), F32),
        ],
        compiler_params=_cparams("parallel", "arbitrary"),
        name="mla_attention",
    )(qfull, qfull, qfull, cos_q, sin_q, kv, kv, proj, cos_k, sin_k, proj)


def _cmp_tokens_kernel(r_ref, w1_ref, pe_ref, b1_ref, w2_ref, o_ref, *, n_cmp):
    r = r_ref[0]
    w1 = w1_ref[0]
    half = w1.shape[0] // 2
    first = _dot(r, w1[:half], HI)
    second = _dot(r, w1[half:], HI)
    rows = r.shape[0]
    second = pltpu.roll(second, rows - 1, 0)
    hid = first + second + _dot(pe_ref[0], w1, HI) + b1_ref[0]
    out = _dot(_silu(hid), w2_ref[0], HI)
    o_ref[0] = jnp.where(_iota((rows, 1), 0) < n_cmp, out, 0.0)


def _compress_tokens(r, w1, pe, b1, w2, n_cmp):
    four, rows, width = r.shape
    hid = w1.shape[2]
    dk = w2.shape[2]
    return pl.pallas_call(
        functools.partial(_cmp_tokens_kernel, n_cmp=n_cmp),
        out_shape=jax.ShapeDtypeStruct((four, rows, dk), F32),
        grid=(four,),
        in_specs=[
            pl.BlockSpec((1, rows, width), lambda i: (i, 0, 0)),
            pl.BlockSpec((1, 2 * width, hid), lambda i: (i // NSA_GROUPS, 0, 0)),
            pl.BlockSpec((1, 1, 2 * width), lambda i: (i // NSA_GROUPS, 0, 0)),
            pl.BlockSpec((1, 1, hid), lambda i: (i // NSA_GROUPS, 0, 0)),
            pl.BlockSpec((1, hid, dk), lambda i: (i // NSA_GROUPS, 0, 0)),
        ],
        out_specs=pl.BlockSpec((1, rows, dk), lambda i: (i, 0, 0)),
        compiler_params=_cparams("parallel"),
        name="nsa_compress_tokens",
    )(r, w1, pe, b1, w2)


def _alibi_slope(h):
    return 2.0 ** (-8.0 * (h + 1) / NSA_HEADS)


def _nsa_cmp_kernel(q_ref, g_ref, cmp_ref, o_ref, sel_ref, *, tq, n_cmp, n_slc):
    qi = pl.program_id(0)
    n_pad = cmp_ref.shape[1]
    scale = NSA_DK ** -0.5
    t_col = qi * tq + _iota((tq, 1), 0)
    n_row = _iota((1, n_pad), 1)
    cmp_end = n_row * CMP_STRIDE + (CMP_LEN - 1)
    valid = (cmp_end <= t_col) & (n_row < n_cmp)
    dist = (t_col - cmp_end).astype(F32)
    gates = jax.nn.sigmoid(g_ref[...])
    q = q_ref[...]
    n_c = _iota((n_pad, SEL_LANES), 0) * CMP_STRIDE
    j_c = _iota((n_pad, SEL_LANES), 1) * SLC_LEN
    overlap = ((n_c < j_c + SLC_LEN) & (n_c + CMP_LEN > j_c)).astype(F32)
    j_row = _iota((1, SEL_LANES), 1)
    cur = lax.shift_right_logical(t_col, int(math.log2(SLC_LEN)))
    forced = (j_row == 0) | (j_row == cur) | (j_row == cur - 1)
    causal = (j_row * SLC_LEN <= t_col) & (j_row < n_slc)
    j_t = _iota((SEL_LANES, tq), 0)
    for g in range(NSA_GROUPS):
        kc = cmp_ref[g].astype(BF16)
        vc = cmp_ref[NSA_GROUPS + g].astype(BF16)
        psum = jnp.zeros((tq, n_pad), F32)
        for hh in range(NSA_HPG):
            h = g * NSA_HPG + hh
            sl = slice(h * NSA_DK, (h + 1) * NSA_DK)
            s = _dot_nt((q[:, sl] * scale).astype(BF16), kc) - _alibi_slope(h) * dist
            s = jnp.where(valid, s, NEG_INF)
            e = jnp.where(valid, jnp.exp(s - jnp.max(s, axis=1, keepdims=True)), 0.0)
            den = jnp.sum(e, axis=1, keepdims=True)
            p = e / jnp.where(den > 0.0, den, 1.0)
            psum = psum + p
            o_ref[:, sl] = _dot(p.astype(BF16), vc) * gates[:, 3 * h:3 * h + 1]
        imp = _dot(psum, overlap, HI)
        imp = jnp.where(forced, FORCE, imp)
        imp = jnp.where(causal, imp, jnp.where(j_row < n_slc, NEG_INF, -3e38))
        imp_t = imp.T
        sel_t = jnp.zeros((SEL_LANES, tq), F32)
        for _ in range(min(SLC_TOPK, n_slc)):
            best = jnp.max(imp_t, axis=0, keepdims=True)
            first = jnp.min(jnp.where(imp_t == best, j_t, SEL_LANES), axis=0, keepdims=True)
            hit = j_t == first
            sel_t = jnp.where(hit, 1.0, sel_t)
            imp_t = jnp.where(hit, -3.4e38, imp_t)
        sel_ref[:, g * SEL_LANES:(g + 1) * SEL_LANES] = sel_t.T.astype(sel_ref.dtype)


def _nsa_compressed(proj, cmp_tokens, n_cmp, n_slc):
    s_len = proj.shape[0]
    tq = NSA_TQ
    hd = NSA_HEADS * NSA_DK
    return pl.pallas_call(
        functools.partial(_nsa_cmp_kernel, tq=tq, n_cmp=n_cmp, n_slc=n_slc),
        out_shape=(jax.ShapeDtypeStruct((s_len, hd), F32),
                   jax.ShapeDtypeStruct((s_len, NSA_GROUPS * SEL_LANES), BF16)),
        grid=(s_len // tq,),
        in_specs=[
            pl.BlockSpec((tq, hd), lambda i: (i, C_Q // hd)),
            pl.BlockSpec((tq, LANES), lambda i: (i, C_G // LANES)),
            pl.BlockSpec(cmp_tokens.shape, lambda i: (0, 0, 0)),
        ],
        out_specs=(pl.BlockSpec((tq, hd), lambda i: (i, 0)),
                   pl.BlockSpec((tq, NSA_GROUPS * SEL_LANES), lambda i: (i, 0))),
        compiler_params=_cparams("parallel"),
        name="nsa_compressed_topk",
    )(proj, proj, cmp_tokens)


def _nsa_tile_range(mode, qi, tq, tk):
    q0 = qi * tq
    last = lax.div(q0 + tq - 1, tk)
    if mode == "slc":
        return jnp.zeros_like(last), last
    return lax.div(jnp.maximum(q0 - (WINDOW - 1), 0), tk), last


def _bf16_parts(x, n=3):
    parts = []
    for _ in range(n):
        hi = float(np.asarray(x, dtype=BF16).astype(np.float32))
        parts.append(hi)
        x = x - hi
    return parts


def _nsa_flash_kernel(*refs, mode, tq, tk, branch):
    if mode == "slc":
        (q_ref, g_ref, k_ref, v_ref, sel_ref, ocmp_ref, owin_ref, z_ref, o_ref,
         q_s, s_s, p_s, a_s, m_s, acc_s) = refs
    else:
        q_ref, g_ref, k_ref, v_ref, o_ref, q_s, s_s, p_s, a_s, m_s, acc_s = refs
    qi = pl.program_id(0)
    step = pl.program_id(1)
    lo, hi = _nsa_tile_range(mode, qi, tq, tk)
    kt = lo + step
    rows = NSA_HPG * tq
    feat_lane = [NSA_DK * (1 - g) for g in range(NSA_GROUPS)]

    @pl.when(step == 0)
    def _():
        lane = _iota((1, NSA_DK), 1)
        for h in range(NSA_HEADS):
            g, hh = divmod(h, NSA_HPG)
            r = slice(hh * tq, (hh + 1) * tq)
            q = q_ref[:, h * NSA_DK:(h + 1) * NSA_DK] * (NSA_DK ** -0.5 * LOG2E)
            q_s[g, r, g * NSA_DK:(g + 1) * NSA_DK] = q.astype(BF16)
            feat = jnp.zeros((1, NSA_DK), F32)
            for i, part in enumerate(3 * _bf16_parts(LOG2E * _alibi_slope(h))):
                feat = jnp.where(lane == i, part, feat)
            q_s[g, r, feat_lane[g]:feat_lane[g] + NSA_DK] = jnp.broadcast_to(feat, (tq, NSA_DK)).astype(BF16)
        m_s[...] = jnp.full(m_s.shape, M_FLOOR, F32)
        acc_s[...] = jnp.zeros(acc_s.shape, F32)

    @pl.when(kt <= hi)
    def _():
        k0 = kt * tk
        delta = (qi * tq + _iota((tq, 1), 0)) - (k0 + _iota((1, tk), 1))
        if mode == "slc":
            blk = lax.div(k0, SLC_LEN) + lax.shift_right_logical(_iota((SEL_LANES, tk), 1), int(math.log2(SLC_LEN)))
            expand = (_iota((SEL_LANES, tk), 0) == blk).astype(BF16)
        else:
            bias = jnp.where((delta >= 0) & (delta < WINDOW), 0.0, NEG_INF)
        jj = _iota((tk, LANES), 0)
        lane = _iota((tk, LANES), 1)
        tile_off = (k0 - qi * tq).astype(F32)
        upper = jnp.where(jj >= 256, 256.0, 0.0)
        lower = (jj & 255).astype(F32)
        kblk = k_ref[...]
        vblk = v_ref[...]
        for g in range(NSA_GROUPS):
            if mode == "slc":
                picked = _dot(sel_ref[:, g * SEL_LANES:(g + 1) * SEL_LANES], expand)
                bias = jnp.where((picked > 0.5) & (delta >= 0), 0.0, NEG_INF)
            f = lane - feat_lane[g]
            data = lax.shift_right_logical(lane, int(math.log2(NSA_DK))) == g
            k_aug = jnp.where(data, kblk, 0.0)
            k_aug = jnp.where((f >= 0) & (f < 3), tile_off, k_aug)
            k_aug = jnp.where((f >= 3) & (f < 6), upper, k_aug)
            k_aug = jnp.where((f >= 6) & (f < 9), lower, k_aug)
            s = _dot_nt(q_s[g], k_aug.astype(BF16))
            for hh in range(NSA_HPG):
                r = slice(hh * tq, (hh + 1) * tq)
                sb = s[r] + bias
                s_s[g, r] = sb
                part = jnp.maximum(jnp.maximum(sb[:, 0:LANES], sb[:, LANES:2 * LANES]),
                                   jnp.maximum(sb[:, 2 * LANES:3 * LANES], sb[:, 3 * LANES:4 * LANES]))
                m_prev = m_s[g, r]
                m_new = jnp.maximum(m_prev, jnp.broadcast_to(jnp.max(part, axis=1, keepdims=True), (tq, LANES)))
                a_s[g, r] = jnp.exp2(m_prev - m_new)
                m_s[g, r] = m_new
        for g in range(NSA_GROUPS):
            for rb in range(rows // SM_ROWS):
                r = slice(rb * SM_ROWS, (rb + 1) * SM_ROWS)
                p_s[g, r] = jnp.exp2(s_s[g, r] - jnp.tile(m_s[g, r], (1, tk // LANES))).astype(BF16)
        for g in range(NSA_GROUPS):
            data = lax.shift_right_logical(lane, int(math.log2(NSA_DK))) == g
            v_aug = jnp.where(lane == feat_lane[g], 1.0, jnp.where(data, vblk, 0.0)).astype(BF16)
            acc_s[g] = a_s[g] * acc_s[g] + _dot(p_s[g], v_aug)

    @pl.when(step == pl.num_programs(1) - 1)
    def _():
        gates = jax.nn.sigmoid(g_ref[...])
        for h in range(NSA_HEADS):
            g, hh = divmod(h, NSA_HPG)
            r = slice(hh * tq, (hh + 1) * tq)
            sl = slice(h * NSA_DK, (h + 1) * NSA_DK)
            total = acc_s[g, r, feat_lane[g]:feat_lane[g] + 1]
            o = acc_s[g, r, g * NSA_DK:(g + 1) * NSA_DK] / total * gates[:, 3 * h + branch:3 * h + branch + 1]
            if mode == "slc":
                o = (o + ocmp_ref[:, sl] + owin_ref[:, sl]) * _silu(z_ref[:, sl])
            o_ref[:, sl] = o.astype(o_ref.dtype)


def _nsa_flash(proj, mode, sel=None, o_cmp=None, o_win=None):
    s_len = proj.shape[0]
    tq, tk = NSA_TQ, min(NSA_TK, s_len)
    assert tk == 4 * LANES
    hd = NSA_HEADS * NSA_DK
    gd = NSA_GROUPS * NSA_DK
    k_slab, v_slab, branch = (2, 3, 1) if mode == "slc" else (4, 5, 2)
    nsteps = s_len // tk if mode == "slc" else -(-(WINDOW - 1) // tk) + 1

    def kv_map(slab):
        def index(qi, step):
            lo, hi = _nsa_tile_range(mode, qi, tq, tk)
            return jnp.minimum(lo + step, hi), C_KV // gd + slab
        return index

    q_map = lambda qi, step: (qi, 0)
    in_specs = [
        pl.BlockSpec((tq, hd), lambda qi, step: (qi, C_Q // hd)),
        pl.BlockSpec((tq, LANES), lambda qi, step: (qi, C_G // LANES)),
        pl.BlockSpec((tk, gd), kv_map(k_slab)),
        pl.BlockSpec((tk, gd), kv_map(v_slab)),
    ]
    args = [proj, proj, proj, proj]
    if mode == "slc":
        in_specs += [
            pl.BlockSpec((tq, NSA_GROUPS * SEL_LANES), q_map),
            pl.BlockSpec((tq, hd), q_map),
            pl.BlockSpec((tq, hd), q_map),
            pl.BlockSpec((tq, hd), lambda qi, step: (qi, C_Z // hd)),
        ]
        args += [sel, o_cmp, o_win, proj]
    rows = NSA_HPG * tq
    return pl.pallas_call(
        functools.partial(_nsa_flash_kernel, mode=mode, tq=tq, tk=tk, branch=branch),
        out_shape=jax.ShapeDtypeStruct((s_len, hd), BF16 if mode == "slc" else F32),
        grid=(s_len // tq, nsteps),
        in_specs=in_specs,
        out_specs=pl.BlockSpec((tq, hd), q_map),
        scratch_shapes=[
            pltpu.VMEM((NSA_GROUPS, rows, LANES), BF16),
            pltpu.VMEM((NSA_GROUPS, rows, tk), F32),
            pltpu.VMEM((NSA_GROUPS, rows, tk), BF16),
            pltpu.VMEM((NSA_GROUPS, rows, LANES), F32),
            pltpu.VMEM((NSA_GROUPS, rows, LANES), F32),
            pltpu.VMEM((NSA_GROUPS, rows, LANES), F32),
        ],
        compiler_params=_cparams("parallel", "arbitrary"),
        name="nsa_selected" if mode == "slc" else "nsa_window",
    )(*args)


def _shift_lerp_kernel(x_ref, xp_ref, g_ref, sc_ref, sh_ref, mu_ref, *o_refs, tm):
    def modulated(x):
        y = x * lax.rsqrt(jnp.mean(x * x, axis=-1, keepdims=True) + EPS) * g_ref[...]
        return y * (1.0 + sc_ref[0, 0:1, :]) + sh_ref[0, 0:1, :]

    h = modulated(x_ref[...])
    before = modulated(xp_ref[...])[7:8, :] * jnp.where(pl.program_id(0) > 0, 1.0, 0.0)
    prev = jnp.where(_iota((tm, 1), 0) == 0, before, pltpu.roll(h, 1, 0))
    xx = prev - h
    for j, o_ref in enumerate(o_refs):
        o_ref[...] = (h + xx * mu_ref[j:j + 1, :]).astype(o_ref.dtype)


def _shift_lerp(x, gain, mod, layer, mu):
    s_len, d = x.shape
    tm = min(ROW_TM, s_len)
    row = pl.BlockSpec((tm, d), lambda i: (i, 0))
    return pl.pallas_call(
        functools.partial(_shift_lerp_kernel, tm=tm),
        out_shape=tuple(jax.ShapeDtypeStruct((s_len, d), BF16) for _ in range(6)),
        grid=(s_len // tm,),
        in_specs=[
            row,
            pl.BlockSpec((8, d), lambda i: (jnp.maximum(i * (tm // 8) - 1, 0), 0)),
            pl.BlockSpec((1, d), lambda i: (0, 0)),
            pl.BlockSpec((1, 8, d), lambda i: (layer, 0, 1)),
            pl.BlockSpec((1, 8, d), lambda i: (layer, 0, 0)),
            pl.BlockSpec((6, d), lambda i: (0, 0)),
        ],
        out_specs=tuple(row for _ in range(6)),
        compiler_params=_cparams("parallel"),
        name="rwkv_shift_lerp",
    )(x, x, gain.reshape(1, d), mod, mod, mu)


def _head_block_diag(value):
    shift = int(math.log2(RWKV_HEAD))
    same = (lax.shift_right_logical(_iota((LANES, LANES), 0), shift)
            == lax.shift_right_logical(_iota((LANES, LANES), 1), shift))
    return jnp.where(same, value, 0.0).astype(F32)


def _rwkv_prep_kernel(k_ref, wl_ref, al_ref, w0_ref, a0_ref, kk_ref, ka_ref, lw_o, kk_o, kka_o, k2_o):
    k = k_ref[...]
    lw_o[...] = -jnp.exp(-_softplus(-(w0_ref[...] + wl_ref[...])) - 0.5)
    a = jax.nn.sigmoid(a0_ref[...] + al_ref[...])
    kkr = k * kk_ref[...]
    ones = _head_block_diag(1.0)
    for j in range(k.shape[1] // LANES):
        sl = slice(j * LANES, (j + 1) * LANES)
        x = kkr[:, sl]
        kk = x / jnp.maximum(jnp.sqrt(_dot(x * x, ones, HI)), 1e-12)
        kk_o[:, sl] = kk
        kka_o[:, sl] = kk * a[:, sl]
    k2_o[...] = k * (1.0 + (a - 1.0) * ka_ref[...])


def _rwkv_prep(k, wl, al, w0, a0, k_k, k_a):
    s_len, d = k.shape
    tm = min(ROW_TM, s_len)
    row = pl.BlockSpec((tm, d), lambda i: (i, 0))
    vec = pl.BlockSpec((1, d), lambda i: (0, 0))
    return pl.pallas_call(
        _rwkv_prep_kernel,
        out_shape=tuple(jax.ShapeDtypeStruct((s_len, d), F32) for _ in range(4)),
        grid=(s_len // tm,),
        in_specs=[row, row, row, vec, vec, vec, vec],
        out_specs=(row, row, row, row),
        compiler_params=_cparams("parallel"),
        name="rwkv_prep",
    )(k, wl, al, w0.reshape(1, d), a0.reshape(1, d), k_k.reshape(1, d), k_a.reshape(1, d))


def _rwkv_chunk_kernel(r_ref, lw_ref, k_ref, v_ref, kk_ref, kka_ref, y_ref, s_s, *, c, hb):
    n = RWKV_HEAD

    @pl.when(pl.program_id(1) == 0)
    def _():
        s_s[...] = jnp.zeros(s_s.shape, F32)

    rr = _iota((c, c), 0)
    cc = _iota((c, c), 1)
    lw = lw_ref[...]
    cum = _dot((rr >= cc).astype(F32), lw, HI)
    grow = jnp.exp(-cum)
    decay = jnp.exp(cum)
    total = decay[c - 1:c, :]
    kk = kk_ref[...]
    q_t = kk * jnp.exp(cum - lw)
    p_t = -(kka_ref[...] * grow)
    k_t = k_ref[...] * grow
    r_t = r_ref[...] * decay
    p_end = p_t * total
    k_end = k_t * total
    r2 = _iota((2 * c, 2 * c), 0)
    c2 = _iota((2 * c, 2 * c), 1) & (c - 1)
    gram_mask = (((r2 < c) & (r2 > c2)) | ((r2 >= c) & ((r2 - c) >= c2))).astype(F32)
    eye = (rr == cc).astype(F32)
    same_block = [(lax.shift_right_logical(rr, s) == lax.shift_right_logical(cc, s)).astype(F32)
                  for s in range(3, int(math.log2(c)) + 1)]
    heads = range(hb)
    sls = [slice(i * n, (i + 1) * n) for i in heads]
    qr = [jnp.concatenate([q_t[:, sl], r_t[:, sl]], axis=0).astype(BF16) for sl in sls]
    pk = [jnp.concatenate([p_t[:, sl], k_t[:, sl]], axis=0).astype(BF16) for sl in sls]
    state = [s_s[i] for i in heads]
    from_state = [_dot_nt(qr[i], state[i].astype(BF16)) for i in heads]
    gram = [_dot_nt(qr[i], pk[i]) * gram_mask for i in heads]
    l_qp = [g[:c, :c] for g in gram]
    vb = [v_ref[:, sl].astype(BF16) for sl in sls]
    from_v = [_dot(jnp.concatenate([gram[i][:c, c:], gram[i][c:, c:]], axis=0).astype(BF16), vb[i]) for i in heads]
    power = [l * same_block[0] for l in l_qp]
    inv = [eye + p for p in power]
    for _ in range(2):
        pb = [p.astype(BF16) for p in power]
        power = [_dot(p, p) for p in pb]
        inv = [x + _dot(x.astype(BF16), p.astype(BF16)) for x, p in zip(inv, power)]
    for lvl in range(1, len(same_block)):
        level_mask = same_block[lvl] - same_block[lvl - 1]
        ib = [x.astype(BF16) for x in inv]
        half = [_dot(ib[i], (l_qp[i] * level_mask).astype(BF16)).astype(BF16) for i in heads]
        inv = [inv[i] + _dot(half[i], ib[i]) for i in heads]
    u = [_dot(inv[i].astype(BF16), (from_state[i][:c] + from_v[i][:c]).astype(BF16)) for i in heads]
    ub = [x.astype(BF16) for x in u]
    for i in heads:
        y_ref[:, sls[i]] = from_state[i][c:] + _dot(gram[i][c:, :c].astype(BF16), ub[i]) + from_v[i][c:]
    for i in heads:
        uv = jnp.concatenate([ub[i], vb[i]], axis=0)
        pk_end = jnp.concatenate([p_end[:, sls[i]], k_end[:, sls[i]]], axis=0).astype(BF16)
        s_s[i] = state[i] * total[:, sls[i]] + _dot_tn(uv, pk_end)


def _rwkv_scan(r, lw, k2, v, kk, kka):
    s_len, d = r.shape
    c, hb = RWKV_CHUNK, RWKV_HB
    width = hb * RWKV_HEAD
    blk = pl.BlockSpec((c, width), lambda hg, ci: (ci, hg))
    return pl.pallas_call(
        functools.partial(_rwkv_chunk_kernel, c=c, hb=hb),
        out_shape=jax.ShapeDtypeStruct((s_len, d), F32),
        grid=(d // width, s_len // c),
        in_specs=[blk] * 6,
        out_specs=blk,
        scratch_shapes=[pltpu.VMEM((hb, RWKV_HEAD, RWKV_HEAD), F32)],
        compiler_params=_cparams("parallel", "arbitrary"),
        name="rwkv_chunk_scan",
    )(r, lw, k2, v, kk, kka)


def _rwkv_post_kernel(y_ref, r_ref, k2_ref, v_ref, z_ref, rk_ref, g_ref, b_ref, o_ref):
    avg = _head_block_diag(1.0 / RWKV_HEAD)
    ones = _head_block_diag(1.0)
    for j in range(y_ref.shape[1] // LANES):
        sl = slice(j * LANES, (j + 1) * LANES)
        y = y_ref[:, sl]
        dev = y - _dot(y, avg, HI)
        var = _dot(dev * dev, avg, HI)
        yn = dev * lax.rsqrt(var + LNX_EPS) * g_ref[:, sl] + b_ref[:, sl]
        bonus = _dot(r_ref[:, sl] * k2_ref[:, sl] * rk_ref[:, sl], ones, HI) * v_ref[:, sl]
        o_ref[:, sl] = ((yn + bonus) * _silu(z_ref[:, sl])).astype(o_ref.dtype)


def _rwkv_post(y, r, k2, v, z, r_k, lnx_g, lnx_b):
    s_len, d = y.shape
    tm = min(ROW_TM, s_len)
    row = pl.BlockSpec((tm, d), lambda i: (i, 0))
    vec = pl.BlockSpec((1, d), lambda i: (0, 0))
    return pl.pallas_call(
        _rwkv_post_kernel,
        out_shape=jax.ShapeDtypeStruct((s_len, d), BF16),
        grid=(s_len // tm,),
        in_specs=[row, row, row, row, row, vec, vec, vec],
        out_specs=row,
        compiler_params=_cparams("parallel"),
        name="rwkv_post",
    )(y, r, k2, v, z, r_k.reshape(1, d), lnx_g.reshape(1, d), lnx_b.reshape(1, d))


def _rmsnorm_kernel(x_ref, g_ref, o_ref):
    x = x_ref[...].astype(F32)
    o_ref[...] = (x * lax.rsqrt(jnp.mean(x * x, axis=-1, keepdims=True) + EPS) * g_ref[...]).astype(o_ref.dtype)


def _rmsnorm(x, gain, out_dtype):
    s_len, d = x.shape
    tm = min(ROW_TM, s_len)
    return pl.pallas_call(
        _rmsnorm_kernel,
        out_shape=jax.ShapeDtypeStruct((s_len, d), out_dtype),
        grid=(s_len // tm,),
        in_specs=[pl.BlockSpec((tm, d), lambda i: (i, 0)), pl.BlockSpec((1, d), lambda i: (0, 0))],
        out_specs=pl.BlockSpec((tm, d), lambda i: (i, 0)),
        compiler_params=_cparams("parallel"),
        name="rmsnorm",
    )(x, gain.reshape(1, d))


def _rotate_half_cols(w):
    half = w.shape[1] // 2
    return jnp.concatenate([-w[:, half:], w[:, :half]], axis=1)


def _arrange_w_in(w_in):
    d = w_in.shape[0]
    q, kv, g, z, qa, kva, mlaz = jnp.split(w_in, [1024, 1792, 1840, 2864, 3376, 3952], axis=1)
    ckv, kpe = kva[:, :KV_LORA], kva[:, KV_LORA:]
    pad = jnp.zeros((d, IN0_COLS - C_G - g.shape[1]), w_in.dtype)
    return jnp.concatenate([q, z, mlaz, qa, ckv, kv, kpe, _rotate_half_cols(kpe), g, pad], axis=1).astype(BF16)


def _arrange_w_qb(w_qb):
    w = w_qb.reshape(Q_LORA, MLA_HEADS, QK_NOPE + QK_ROPE)
    nope = w[:, :, :QK_NOPE].reshape(Q_LORA, -1)
    pe = w[:, :, QK_NOPE:]
    half = QK_ROPE // 2
    rot = jnp.concatenate([-pe[:, :, half:], pe[:, :, :half]], axis=2)
    return jnp.concatenate([nope, pe.reshape(Q_LORA, -1), rot.reshape(Q_LORA, -1)], axis=1).astype(BF16)


def _arrange_w_kvb(w_kvb):
    w = w_kvb.reshape(KV_LORA, MLA_HEADS, QK_NOPE + V_HEAD)
    return jnp.concatenate([w[:, :, :QK_NOPE].reshape(KV_LORA, -1), w[:, :, QK_NOPE:].reshape(KV_LORA, -1)],
                           axis=1).astype(BF16)


def _rope_tables(s_len):
    inv = ROPE_THETA ** (-jnp.arange(0, QK_ROPE, 2, dtype=F32) / QK_ROPE)
    ang = jnp.arange(s_len, dtype=F32)[:, None] * inv[None]
    cos = jnp.concatenate([jnp.cos(ang), jnp.cos(ang)], axis=1)
    sin = jnp.concatenate([jnp.sin(ang), jnp.sin(ang)], axis=1)
    return cos, sin


def _pad_cols(w, n):
    return jnp.pad(w, ((0, 0), (0, n - w.shape[1])))


def _pad_rows(w, n):
    return jnp.pad(w, ((0, n - w.shape[0]), (0, 0)))


def _attention_layer(x, mod, layer, gain, w_in, w_out, pe_k, w1_k, b1_k, w2_k, pe_v, w1_v, b1_v, w2_v,
                     qa_g, w_qb, kva_g, w_kvb):
    s_len, d = x.shape
    assert s_len % 1024 == 0 and s_len // SLC_LEN <= SEL_LANES
    n_cmp = (s_len - CMP_LEN) // CMP_STRIDE + 1
    n_slc = s_len // SLC_LEN
    proj = _matmul(x, _arrange_w_in(w_in), out_dtype=F32, prologue="rms_mod", gain=gain, mod=mod, layer=layer,
                   name="in_proj")
    qfull = _matmul(proj, _arrange_w_qb(w_qb), out_dtype=BF16, a_col=C_QA // Q_LORA, k=Q_LORA, prologue="rms",
                    gain=qa_g, name="mla_q_proj")
    kv = _matmul(proj, _arrange_w_kvb(w_kvb), out_dtype=BF16, a_col=C_CKV // KV_LORA, k=KV_LORA, prologue="rms",
                 gain=kva_g, name="mla_kv_proj")
    cos, sin = _rope_tables(s_len)
    y_mla = _mla_attention(qfull, kv, proj, jnp.tile(cos, (1, MLA_HEADS)), jnp.tile(sin, (1, MLA_HEADS)), cos, sin)
    cl = CMP_LEN * NSA_DK
    kcvc = proj[:, C_KV:C_KV + 2 * NSA_GROUPS * NSA_DK].reshape(s_len, 2, NSA_GROUPS, NSA_DK)
    r = kcvc.transpose(1, 2, 0, 3).reshape(2 * NSA_GROUPS, s_len // CMP_STRIDE, CMP_STRIDE * NSA_DK)
    cmp_tokens = _compress_tokens(
        r,
        jnp.stack([w1_k.reshape(cl, -1), w1_v.reshape(cl, -1)]),
        jnp.stack([pe_k.reshape(1, cl), pe_v.reshape(1, cl)]),
        jnp.stack([b1_k.reshape(1, -1), b1_v.reshape(1, -1)]),
        jnp.stack([w2_k, w2_v]),
        n_cmp)
    o_cmp, sel = _nsa_compressed(proj, cmp_tokens, n_cmp, n_slc)
    o_win = _nsa_flash(proj, "win")
    y_nsa = _nsa_flash(proj, "slc", sel=sel, o_cmp=o_cmp, o_win=o_win)
    y = jnp.concatenate([y_nsa, y_mla], axis=1)
    return _matmul(y, w_out.astype(BF16), out_dtype=F32, epilogue="residual", res=x, mod=mod, layer=layer,
                   name="attn_out_proj")


def _rwkv_layer(x, mod, layer, gain, mu, w_r, w_k, w_v, w_z, w_o, w0, w1, w2, a0, a1, a2, k_k, k_a, r_k,
                lnx_g, lnx_b):
    xr, xw, xk, xv, xa, xz = _shift_lerp(x, gain, mod, layer, mu)
    r = _matmul(xr, w_r.astype(BF16), out_dtype=F32, name="rwkv_r")
    k = _matmul(xk, w_k.astype(BF16), out_dtype=F32, name="rwkv_k")
    v = _matmul(xv, w_v.astype(BF16), out_dtype=F32, name="rwkv_v")
    z = _matmul(xz, w_z.astype(BF16), out_dtype=F32, name="rwkv_z")
    wl = _matmul(xw, _pad_cols(w1, LORA_PAD).astype(BF16), out_dtype=BF16, epilogue="tanh", name="rwkv_w_lora_a")
    wl = _matmul(wl, _pad_rows(w2, LORA_PAD).astype(BF16), out_dtype=F32, name="rwkv_w_lora_b")
    al = _matmul(xa, _pad_cols(a1, LORA_PAD).astype(BF16), out_dtype=BF16, name="rwkv_a_lora_a")
    al = _matmul(al, _pad_rows(a2, LORA_PAD).astype(BF16), out_dtype=F32, name="rwkv_a_lora_b")
    lw, kk, kka, k2 = _rwkv_prep(k, wl, al, w0, a0, k_k, k_a)
    y = _rwkv_scan(r, lw, k2, v, kk, kka)
    y = _rwkv_post(y, r, k2, v, z, r_k, lnx_g, lnx_b)
    return _matmul(y, w_o.astype(BF16), out_dtype=F32, epilogue="residual", res=x, mod=mod, layer=layer,
                   name="rwkv_out_proj")


def kernel(x, c, norm_g, ada_w, ada_b, final_g, a_w_in, a_w_out, nsa_pe_k, nsa_w1_k, nsa_b1_k, nsa_w2_k, nsa_pe_v, nsa_w1_v, nsa_b1_v, nsa_w2_v, mla_qa_g, mla_w_qb, mla_kva_g, mla_w_kvb, r_mu, r_w_r, r_w_k, r_w_v, r_w_z, r_w_o, r_w0, r_w1, r_w2, r_a0, r_a1, r_a2, r_k_k, r_k_a, r_r_k, r_lnx_g, r_lnx_b):
    batch, s_len, d = x.shape
    depth = ada_w.shape[0]
    outs = []
    for b in range(batch):
        mod = _modulation(c[b:b + 1], ada_w, ada_b)
        xb = x[b]
        for i in range(depth):
            j = i // 2
            if i % 2 == 0:
                xb = _attention_layer(xb, mod, i, norm_g[i], a_w_in[j], a_w_out[j],
                                      nsa_pe_k[j], nsa_w1_k[j], nsa_b1_k[j], nsa_w2_k[j],
                                      nsa_pe_v[j], nsa_w1_v[j], nsa_b1_v[j], nsa_w2_v[j],
                                      mla_qa_g[j], mla_w_qb[j], mla_kva_g[j], mla_w_kvb[j])
            else:
                xb = _rwkv_layer(xb, mod, i, norm_g[i], r_mu[j], r_w_r[j], r_w_k[j], r_w_v[j], r_w_z[j], r_w_o[j],
                                 r_w0[j], r_w1[j], r_w2[j], r_a0[j], r_a1[j], r_a2[j], r_k_k[j], r_k_a[j],
                                 r_r_k[j], r_lnx_g[j], r_lnx_b[j])
        outs.append(_rmsnorm(xb, final_g, x.dtype))
    return jnp.stack(outs)
```

```python
import functools
import math

import jax
import jax.numpy as jnp
import numpy as np
from jax import lax
from jax.experimental import pallas as pl
from jax.experimental.pallas import tpu as pltpu

F32 = jnp.float32
BF16 = jnp.bfloat16
HI = lax.Precision.HIGHEST

EPS = 1e-6
NEG_INF = -1e30
FORCE = 1e9
NSA_HEADS = 16
NSA_GROUPS = 2
NSA_HPG = NSA_HEADS // NSA_GROUPS
NSA_DK = 64
CMP_LEN = 32
CMP_STRIDE = 16
SLC_LEN = 64
SLC_TOPK = 16
WINDOW = 512
MLA_HEADS = 8
Q_LORA = 512
KV_LORA = 512
QK_NOPE = 128
QK_ROPE = 64
V_HEAD = 128
ROPE_THETA = 10000.0
RWKV_HEAD = 64
LORA_PAD = 128
LNX_EPS = 64e-5

LANES = 128
VMEM_LIMIT_BYTES = 48 * 1024 * 1024

MM_TM = 1024
MM_TM_F32 = 512
MM_TN = 1024
IN_PROJ_TM = 1024
OUT_RMS_TM = 512
ROW_TM = 256
MLA_T = 512
NSA_TQ = 128
NSA_TK = 512
RWKV_CHUNK = 64
RWKV_HB = 32
SEL_LANES = 128
SM_ROWS = 64

LOG2E = 1.0 / math.log(2.0)
M_FLOOR = -1e20

C_Q = 0
C_Z = 1024
C_MLAZ = 2048
C_QA = 3072
C_CKV = 3584
C_KV = 4096
C_KPE = 4864
C_G = 4992
IN0_COLS = 5120


def _cparams(*sem):
    return pltpu.CompilerParams(dimension_semantics=sem, vmem_limit_bytes=VMEM_LIMIT_BYTES)


def _dot(a, b, precision=None):
    return jnp.dot(a, b, preferred_element_type=F32, precision=precision)


def _dot_nt(a, b):
    return lax.dot_general(a, b, (((1,), (1,)), ((), ())), preferred_element_type=F32)


def _dot_tn(a, b):
    return lax.dot_general(a, b, (((0,), (0,)), ((), ())), preferred_element_type=F32)


def _silu(x):
    return x * jax.nn.sigmoid(x)


def _iota(shape, dim):
    return lax.broadcasted_iota(jnp.int32, shape, dim)


def _lane_tile_reduce(x, op):
    parts = [x[:, i * LANES:(i + 1) * LANES] for i in range(x.shape[1] // LANES)]
    while len(parts) > 1:
        parts = [op(parts[i], parts[i + 1]) if i + 1 < len(parts) else parts[i] for i in range(0, len(parts), 2)]
    return parts[0]


def _mod_kernel(c_ref, w_ref, b_ref, o_ref):
    c = c_ref[...]
    sc = jnp.broadcast_to(_silu(c), (8, c.shape[1]))
    o_ref[0] = _dot(sc, w_ref[0], HI) + b_ref[0]


def _modulation(c, ada_w, ada_b):
    depth, d, n = ada_w.shape
    tn = 768
    return pl.pallas_call(
        _mod_kernel,
        out_shape=jax.ShapeDtypeStruct((depth, 8, n), F32),
        grid=(depth, n // tn),
        in_specs=[
            pl.BlockSpec((1, d), lambda i, j: (0, 0)),
            pl.BlockSpec((1, d, tn), lambda i, j: (i, 0, j)),
            pl.BlockSpec((1, 1, tn), lambda i, j: (i, 0, j)),
        ],
        out_specs=pl.BlockSpec((1, 8, tn), lambda i, j: (i, 0, j)),
        compiler_params=_cparams("parallel", "parallel"),
        name="adaln_mod",
    )(c, ada_w, ada_b.reshape(depth, 1, n))


def _mm_kernel(*refs, prologue, epilogue, split_k):
    it = iter(refs)
    a_ref = next(it)
    a2_ref = next(it) if split_k else None
    b_ref = next(it)
    g_ref = next(it) if prologue != "none" else None
    sc_ref, sh_ref = (next(it), next(it)) if prologue == "rms_mod" else (None, None)
    res_ref, gate_ref = (next(it), next(it)) if epilogue.startswith("residual") else (None, None)
    fg_ref = next(it) if epilogue == "residual_rms" else None
    o_ref = next(it)
    if prologue == "none":
        a = a_ref[...]
    else:
        a_s = next(it)

        @pl.when(pl.program_id(1) == 0)
        def _():
            x = a_ref[...].astype(F32)
            y = x * lax.rsqrt(jnp.mean(x * x, axis=-1, keepdims=True) + EPS) * g_ref[...]
            if prologue == "rms_mod":
                y = y * (1.0 + sc_ref[0, 0:1, :]) + sh_ref[0, 0:1, :]
            a_s[...] = y.astype(BF16)

        a = a_s[...]
    if split_k:
        k1 = a.shape[1]
        acc = _dot(a, b_ref[0:k1, :]) + _dot(a2_ref[...], b_ref[k1:, :])
    else:
        acc = _dot(a, b_ref[...])
    if epilogue == "tanh":
        acc = jnp.tanh(acc)
    elif epilogue.startswith("residual"):
        acc = res_ref[...] + gate_ref[0, 0:1, :] * acc
        if epilogue == "residual_rms":
            acc = acc * lax.rsqrt(jnp.mean(acc * acc, axis=-1, keepdims=True) + EPS) * fg_ref[...]
    o_ref[...] = acc.astype(o_ref.dtype)


def _matmul(a, b, *, out_dtype, a2=None, a_col=0, k=None, prologue="none", gain=None, mod=None, layer=0,
            epilogue="none", res=None, final_gain=None, tm=None, tn=None, name="matmul"):
    m = a.shape[0]
    n = b.shape[1]
    k = (b.shape[0] if a2 is None else a.shape[1]) if k is None else k
    d_mod = None if mod is None else mod.shape[2] // 3
    tm = min(tm or (MM_TM if prologue == "none" else MM_TM_F32), m)
    tn = n if epilogue == "residual_rms" else min(tn or MM_TN, n)
    in_specs = [pl.BlockSpec((tm, k), lambda i, j: (i, a_col))]
    args = [a]
    if a2 is not None:
        in_specs.append(pl.BlockSpec((tm, a2.shape[1]), lambda i, j: (i, 0)))
        args.append(a2)
    in_specs.append(pl.BlockSpec((b.shape[0], tn), lambda i, j: (0, j)))
    args.append(b)
    scratch = []
    if prologue != "none":
        in_specs.append(pl.BlockSpec((1, k), lambda i, j: (0, 0)))
        args.append(gain.reshape(1, k).astype(F32))
        scratch.append(pltpu.VMEM((tm, k), BF16))
    if prologue == "rms_mod":
        in_specs += [pl.BlockSpec((1, 8, k), lambda i, j: (layer, 0, 1)), pl.BlockSpec((1, 8, k), lambda i, j: (layer, 0, 0))]
        args += [mod, mod]
    if epilogue.startswith("residual"):
        in_specs += [pl.BlockSpec((tm, tn), lambda i, j: (i, j)),
                     pl.BlockSpec((1, 8, tn), lambda i, j: (layer, 0, 2 * (d_mod // tn) + j))]
        args += [res, mod]
    if epilogue == "residual_rms":
        in_specs.append(pl.BlockSpec((1, n), lambda i, j: (0, 0)))
        args.append(final_gain.reshape(1, n))
    return pl.pallas_call(
        functools.partial(_mm_kernel, prologue=prologue, epilogue=epilogue, split_k=a2 is not None),
        out_shape=jax.ShapeDtypeStruct((m, n), out_dtype),
        grid=(m // tm, n // tn),
        in_specs=in_specs,
        out_specs=pl.BlockSpec((tm, tn), lambda i, j: (i, j)),
        scratch_shapes=scratch,
        compiler_params=_cparams("parallel", "arbitrary"),
        name=name,
    )(*args)


def _mla_attn_kernel(qn_ref, qp_ref, qpr_ref, cq_ref, sq_ref, k_ref, v_ref, kpe_ref, ck_ref, sk_ref, z_ref,
                     o_ref, q_s, kc_s, s_s, p_s, a_s, m_s, l_s, acc_s, *, t, scale):
    qi = pl.program_id(0)
    ki = pl.program_id(1)
    dq = QK_NOPE + QK_ROPE

    @pl.when(ki == 0)
    def _():
        qn = (qn_ref[...].astype(F32) * (scale * LOG2E)).astype(BF16)
        cos = cq_ref[...] * (scale * LOG2E)
        sin = sq_ref[...] * (scale * LOG2E)
        for h in range(MLA_HEADS):
            pe = slice(h * QK_ROPE, (h + 1) * QK_ROPE)
            qp = qp_ref[:, pe].astype(F32) * cos + qpr_ref[:, pe].astype(F32) * sin
            q_s[:, h * dq:h * dq + QK_NOPE] = qn[:, h * QK_NOPE:(h + 1) * QK_NOPE]
            q_s[:, h * dq + QK_NOPE:(h + 1) * dq] = qp.astype(BF16)
        m_s[...] = jnp.full(m_s.shape, M_FLOOR, F32)
        l_s[...] = jnp.zeros(l_s.shape, F32)
        acc_s[...] = jnp.zeros(acc_s.shape, F32)

    def step(diagonal):
        kpe = kpe_ref[...]
        kp = (kpe[:, :QK_ROPE] * ck_ref[...] + kpe[:, QK_ROPE:] * sk_ref[...]).astype(BF16)
        for h in range(MLA_HEADS):
            kc_s[h, :, 0:QK_NOPE] = k_ref[:, h * QK_NOPE:(h + 1) * QK_NOPE]
            kc_s[h, :, QK_NOPE:dq] = kp
        for h in range(MLA_HEADS):
            s = _dot_nt(q_s[:, h * dq:(h + 1) * dq], kc_s[h])
            if diagonal:
                s = jnp.where(_iota((t, 1), 0) >= _iota((1, t), 1), s, NEG_INF)
            s_s[h] = s
            row_max = jnp.max(_lane_tile_reduce(s, jnp.maximum), axis=1, keepdims=True)
            m_prev = m_s[h]
            m_new = jnp.maximum(m_prev, jnp.broadcast_to(row_max, (t, LANES)))
            a_s[h] = jnp.exp2(m_prev - m_new)
            m_s[h] = m_new
        for h in range(MLA_HEADS):
            for rb in range(t // SM_ROWS):
                r = slice(rb * SM_ROWS, (rb + 1) * SM_ROWS)
                p = jnp.exp2(s_s[h, r] - jnp.tile(m_s[h, r], (1, t // LANES)))
                p_s[h, r] = p.astype(BF16)
                row_sum = jnp.sum(_lane_tile_reduce(p, jnp.add), axis=1, keepdims=True)
                l_s[h, r] = a_s[h, r] * l_s[h, r] + jnp.broadcast_to(row_sum, (SM_ROWS, LANES))
        for h in range(MLA_HEADS):
            acc_s[h] = a_s[h] * acc_s[h] + _dot(p_s[h], v_ref[:, h * V_HEAD:(h + 1) * V_HEAD])

    @pl.when(ki < qi)
    def _():
        step(False)

    @pl.when(ki == qi)
    def _():
        step(True)
        for h in range(MLA_HEADS):
            sl = slice(h * V_HEAD, (h + 1) * V_HEAD)
            o_ref[:, sl] = (acc_s[h] / l_s[h] * _silu(z_ref[:, sl])).astype(o_ref.dtype)


def _mla_attention(qfull, kv, proj, cos, sin):
    s_len = qfull.shape[0]
    t = min(MLA_T, s_len)
    n = s_len // t
    hn = MLA_HEADS * QK_NOPE
    hp = MLA_HEADS * QK_ROPE
    hv = MLA_HEADS * V_HEAD
    kv_idx = lambda qi, ki: (jnp.minimum(ki, qi), 0)
    return pl.pallas_call(
        functools.partial(_mla_attn_kernel, t=t, scale=(QK_NOPE + QK_ROPE) ** -0.5),
        out_shape=jax.ShapeDtypeStruct((s_len, hv), BF16),
        grid=(n, n),
        in_specs=[
            pl.BlockSpec((t, hn), lambda qi, ki: (qi, 0)),
            pl.BlockSpec((t, hp), lambda qi, ki: (qi, hn // hp)),
            pl.BlockSpec((t, hp), lambda qi, ki: (qi, hn // hp + 1)),
            pl.BlockSpec((t, QK_ROPE), lambda qi, ki: (qi, 0)),
            pl.BlockSpec((t, QK_ROPE), lambda qi, ki: (qi, 0)),
            pl.BlockSpec((t, hn), kv_idx),
            pl.BlockSpec((t, hv), lambda qi, ki: (jnp.minimum(ki, qi), 1)),
            pl.BlockSpec((t, 2 * QK_ROPE), lambda qi, ki: (jnp.minimum(ki, qi), C_KPE // (2 * QK_ROPE))),
            pl.BlockSpec((t, QK_ROPE), kv_idx),
            pl.BlockSpec((t, QK_ROPE), kv_idx),
            pl.BlockSpec((t, hv), lambda qi, ki: (qi, C_MLAZ // hv)),
        ],
        out_specs=pl.BlockSpec((t, hv), lambda qi, ki: (qi, 0)),
        scratch_shapes=[
            pltpu.VMEM((t, hn + hp), BF16),
            pltpu.VMEM((MLA_HEADS, t, QK_NOPE + QK_ROPE), BF16),
            pltpu.VMEM((MLA_HEADS, t, t), F32),
            pltpu.VMEM((MLA_HEADS, t, t), BF16),
            pltpu.VMEM((MLA_HEADS, t, LANES), F32),
            pltpu.VMEM((MLA_HEADS, t, LANES), F32),
            pltpu.VMEM((MLA_HEADS, t, LANES), F32),
            pltpu.VMEM((MLA_HEADS, t, V_HEAD), F32),
        ],
        compiler_params=_cparams("parallel", "arbitrary"),
        name="mla_attention",
    )(qfull, qfull, qfull, cos, sin, kv, kv, proj, cos, sin, proj)


def _cmp_tokens_kernel(r_ref, w1_ref, pe_ref, b1_ref, w2_ref, o_ref, *, n_cmp):
    r = r_ref[0]
    w1 = w1_ref[0]
    half = w1.shape[0] // 2
    first = _dot(r, w1[:half], HI)
    second = _dot(r, w1[half:], HI)
    rows = r.shape[0]
    second = pltpu.roll(second, rows - 1, 0)
    hid = first + second + _dot(pe_ref[0], w1, HI) + b1_ref[0]
    out = _dot(_silu(hid), w2_ref[0], HI)
    o_ref[0] = jnp.where(_iota((rows, 1), 0) < n_cmp, out, 0.0)


def _compress_tokens(r, w1, pe, b1, w2, n_cmp):
    four, rows, width = r.shape
    hid = w1.shape[2]
    dk = w2.shape[2]
    return pl.pallas_call(
        functools.partial(_cmp_tokens_kernel, n_cmp=n_cmp),
        out_shape=jax.ShapeDtypeStruct((four, rows, dk), F32),
        grid=(four,),
        in_specs=[
            pl.BlockSpec((1, rows, width), lambda i: (i, 0, 0)),
            pl.BlockSpec((1, 2 * width, hid), lambda i: (i // NSA_GROUPS, 0, 0)),
            pl.BlockSpec((1, 1, 2 * width), lambda i: (i // NSA_GROUPS, 0, 0)),
            pl.BlockSpec((1, 1, hid), lambda i: (i // NSA_GROUPS, 0, 0)),
            pl.BlockSpec((1, hid, dk), lambda i: (i // NSA_GROUPS, 0, 0)),
        ],
        out_specs=pl.BlockSpec((1, rows, dk), lambda i: (i, 0, 0)),
        compiler_params=_cparams("parallel"),
        name="nsa_compress_tokens",
    )(r, w1, pe, b1, w2)


def _alibi_slope(h):
    return 2.0 ** (-8.0 * (h + 1) / NSA_HEADS)


def _bf16_parts(x, n=3):
    parts = []
    for _ in range(n):
        hi = float(np.asarray(x, dtype=BF16).astype(np.float32))
        parts.append(hi)
        x = x - hi
    return parts


def _nsa_cmp_kernel(q_ref, g_ref, cmp_ref, o_ref, sel_ref, any_ref, qa_s, ka_s, va_s, s_s, m_s, ehi_s, elo_s,
                    *, tq, n_cmp, n_slc):
    qi = pl.program_id(0)
    n_pad = cmp_ref.shape[1]
    t_col = qi * tq + _iota((tq, 1), 0)
    n_row = _iota((1, n_pad), 1)
    cmp_end = n_row * CMP_STRIDE + (CMP_LEN - 1)
    bias = jnp.where((cmp_end <= t_col) & (n_row < n_cmp), 0.0, NEG_INF)
    gates = jax.nn.sigmoid(g_ref[...])
    lane = _iota((n_pad, NSA_DK), 1)
    pos = _iota((n_pad, NSA_DK), 0) * CMP_STRIDE + (CMP_LEN - 1)
    upper = (pos & ~255).astype(F32)
    lower = (pos & 255).astype(F32)
    feat = jnp.where(lane < 3, upper, jnp.where(lane < 6, lower, 0.0)).astype(BF16)
    ones_col = jnp.where(lane == 0, 1.0, 0.0).astype(BF16)
    for g in range(NSA_GROUPS):
        ka_s[g, :, 0:NSA_DK] = cmp_ref[g].astype(BF16)
        ka_s[g, :, NSA_DK:2 * NSA_DK] = feat
        va_s[g, :, 0:NSA_DK] = cmp_ref[NSA_GROUPS + g].astype(BF16)
        va_s[g, :, NSA_DK:2 * NSA_DK] = ones_col
    qlane = _iota((1, NSA_DK), 1)
    for h in range(NSA_HEADS):
        qa_s[h, :, 0:NSA_DK] = (q_ref[:, h * NSA_DK:(h + 1) * NSA_DK] * (NSA_DK ** -0.5 * LOG2E)).astype(BF16)
        qfeat = jnp.zeros((1, NSA_DK), F32)
        for i, part in enumerate(2 * _bf16_parts(LOG2E * _alibi_slope(h))):
            qfeat = jnp.where(qlane == i, part, qfeat)
        qa_s[h, :, NSA_DK:2 * NSA_DK] = jnp.broadcast_to(qfeat, (tq, NSA_DK)).astype(BF16)
    n_c = _iota((n_pad, SEL_LANES), 0) * CMP_STRIDE
    j_c = _iota((n_pad, SEL_LANES), 1) * SLC_LEN
    overlap = ((n_c < j_c + SLC_LEN) & (n_c + CMP_LEN > j_c)).astype(BF16)
    j_row = _iota((1, SEL_LANES), 1)
    cur = lax.shift_right_logical(t_col, int(math.log2(SLC_LEN)))
    forced = (j_row == 0) | (j_row == cur) | (j_row == cur - 1)
    causal = (j_row * SLC_LEN <= t_col) & (j_row < n_slc)
    j_t = _iota((SEL_LANES, tq), 0)
    for h in range(NSA_HEADS):
        s = _dot_nt(qa_s[h], ka_s[h // NSA_HPG]) + bias
        s_s[h] = s
        row_max = jnp.max(_lane_tile_reduce(s, jnp.maximum), axis=1, keepdims=True)
        m_s[h] = jnp.maximum(jnp.broadcast_to(row_max, (tq, LANES)), M_FLOOR)
    for h in range(NSA_HEADS):
        e = jnp.exp2(s_s[h] - jnp.tile(m_s[h], (1, n_pad // LANES)))
        e_hi = e.astype(BF16)
        ehi_s[h] = e_hi
        elo_s[h] = (e - e_hi.astype(F32)).astype(BF16)
    for g in range(NSA_GROUPS):
        imp = jnp.zeros((tq, SEL_LANES), F32)
        for hh in range(NSA_HPG):
            h = g * NSA_HPG + hh
            sl = slice(h * NSA_DK, (h + 1) * NSA_DK)
            pv = _dot(ehi_s[h], va_s[g]) + _dot(elo_s[h], va_s[g])
            den = pv[:, NSA_DK:NSA_DK + 1]
            inv_den = 1.0 / jnp.where(den > 0.0, den, 1.0)
            o_ref[:, sl] = pv[:, 0:NSA_DK] * (inv_den * gates[:, 3 * h:3 * h + 1])
            imp = imp + (_dot(ehi_s[h], overlap) + _dot(elo_s[h], overlap)) * inv_den
        imp = jnp.where(forced, FORCE, imp)
        imp = jnp.where(causal, imp, jnp.where(j_row < n_slc, NEG_INF, -3e38))
        imp_t = imp.T
        sel_t = jnp.zeros((SEL_LANES, tq), F32)
        for _ in range(min(SLC_TOPK, n_slc)):
            best = jnp.max(imp_t, axis=0, keepdims=True)
            first = jnp.min(jnp.where(imp_t == best, j_t, SEL_LANES), axis=0, keepdims=True)
            hit = j_t == first
            sel_t = jnp.where(hit, 1.0, sel_t)
            imp_t = jnp.where(hit, -3.4e38, imp_t)
        sel = sel_t.T
        sel_ref[:, g * SEL_LANES:(g + 1) * SEL_LANES] = sel.astype(sel_ref.dtype)
        tile_any = jnp.max(sel, axis=0, keepdims=True)
        picked_any = tile_any if g == 0 else jnp.maximum(picked_any, tile_any)
    any_ref[0] = jnp.broadcast_to(picked_any, (8, SEL_LANES))


def _nsa_compressed(proj, cmp_tokens, n_cmp, n_slc):
    s_len = proj.shape[0]
    tq = NSA_TQ
    hd = NSA_HEADS * NSA_DK
    return pl.pallas_call(
        functools.partial(_nsa_cmp_kernel, tq=tq, n_cmp=n_cmp, n_slc=n_slc),
        out_shape=(jax.ShapeDtypeStruct((s_len, hd), F32),
                   jax.ShapeDtypeStruct((s_len, NSA_GROUPS * SEL_LANES), BF16),
                   jax.ShapeDtypeStruct((s_len // tq, 8, SEL_LANES), F32)),
        grid=(s_len // tq,),
        in_specs=[
            pl.BlockSpec((tq, hd), lambda i: (i, C_Q // hd)),
            pl.BlockSpec((tq, LANES), lambda i: (i, C_G // LANES)),
            pl.BlockSpec(cmp_tokens.shape, lambda i: (0, 0, 0)),
        ],
        out_specs=(pl.BlockSpec((tq, hd), lambda i: (i, 0)),
                   pl.BlockSpec((tq, NSA_GROUPS * SEL_LANES), lambda i: (i, 0)),
                   pl.BlockSpec((1, 8, SEL_LANES), lambda i: (i, 0, 0))),
        scratch_shapes=[
            pltpu.VMEM((NSA_HEADS, tq, 2 * NSA_DK), BF16),
            pltpu.VMEM((NSA_GROUPS, cmp_tokens.shape[1], 2 * NSA_DK), BF16),
            pltpu.VMEM((NSA_GROUPS, cmp_tokens.shape[1], 2 * NSA_DK), BF16),
            pltpu.VMEM((NSA_HEADS, tq, cmp_tokens.shape[1]), F32),
            pltpu.VMEM((NSA_HEADS, tq, LANES), F32),
            pltpu.VMEM((NSA_HEADS, tq, cmp_tokens.shape[1]), BF16),
            pltpu.VMEM((NSA_HEADS, tq, cmp_tokens.shape[1]), BF16),
        ],
        compiler_params=_cparams("parallel"),
        name="nsa_compressed_topk",
    )(proj, proj, cmp_tokens)


def _nsa_tile_range(mode, qi, tq, tk):
    q0 = qi * tq
    last = lax.div(q0 + tq - 1, tk)
    if mode == "slc":
        return jnp.zeros_like(last), last
    return lax.div(jnp.maximum(q0 - (WINDOW - 1), 0), tk), last


def _nsa_flash_kernel(*refs, mode, tq, tk, branch):
    if mode == "slc":
        (used_ref, _, q_ref, g_ref, k_ref, v_ref, sel_ref, ocmp_ref, owin_ref, z_ref, o_ref,
         q_s, s_s, p_s, a_s, m_s, acc_s) = refs
    else:
        q_ref, g_ref, k_ref, v_ref, o_ref, q_s, s_s, p_s, a_s, m_s, acc_s = refs
    qi = pl.program_id(0)
    step = pl.program_id(1)
    lo, hi = _nsa_tile_range(mode, qi, tq, tk)
    kt = lo + step
    active = kt <= hi
    if mode == "slc":
        active = active & (used_ref[qi * pl.num_programs(1) + step] != 0)
    rows = NSA_HPG * tq
    feat_lane = [NSA_DK * (1 - g) for g in range(NSA_GROUPS)]

    @pl.when(step == 0)
    def _():
        lane = _iota((1, NSA_DK), 1)
        for h in range(NSA_HEADS):
            g, hh = divmod(h, NSA_HPG)
            r = slice(hh * tq, (hh + 1) * tq)
            q = q_ref[:, h * NSA_DK:(h + 1) * NSA_DK] * (NSA_DK ** -0.5 * LOG2E)
            q_s[g, r, g * NSA_DK:(g + 1) * NSA_DK] = q.astype(BF16)
            feat = jnp.zeros((1, NSA_DK), F32)
            for i, part in enumerate(3 * _bf16_parts(LOG2E * _alibi_slope(h))):
                feat = jnp.where(lane == i, part, feat)
            q_s[g, r, feat_lane[g]:feat_lane[g] + NSA_DK] = jnp.broadcast_to(feat, (tq, NSA_DK)).astype(BF16)
        m_s[...] = jnp.full(m_s.shape, M_FLOOR, F32)
        acc_s[...] = jnp.zeros(acc_s.shape, F32)

    @pl.when(active)
    def _():
        k0 = kt * tk
        delta = (qi * tq + _iota((tq, 1), 0)) - (k0 + _iota((1, tk), 1))
        if mode == "slc":
            blk = lax.div(k0, SLC_LEN) + lax.shift_right_logical(_iota((SEL_LANES, tk), 1), int(math.log2(SLC_LEN)))
            expand = (_iota((SEL_LANES, tk), 0) == blk).astype(BF16)
        else:
            bias = jnp.where((delta >= 0) & (delta < WINDOW), 0.0, NEG_INF)
        jj = _iota((tk, LANES), 0)
        lane = _iota((tk, LANES), 1)
        tile_off = (k0 - qi * tq).astype(F32)
        upper = jnp.where(jj >= 256, 256.0, 0.0)
        lower = (jj & 255).astype(F32)
        kblk = k_ref[...]
        vblk = v_ref[...]
        for g in range(NSA_GROUPS):
            if mode == "slc":
                picked = _dot(sel_ref[:, g * SEL_LANES:(g + 1) * SEL_LANES], expand)
                bias = jnp.where((picked > 0.5) & (delta >= 0), 0.0, NEG_INF)
            f = lane - feat_lane[g]
            data = lax.shift_right_logical(lane, int(math.log2(NSA_DK))) == g
            k_aug = jnp.where(data, kblk, 0.0)
            k_aug = jnp.where((f >= 0) & (f < 3), tile_off, k_aug)
            k_aug = jnp.where((f >= 3) & (f < 6), upper, k_aug)
            k_aug = jnp.where((f >= 6) & (f < 9), lower, k_aug)
            s = _dot_nt(q_s[g], k_aug.astype(BF16))
            for hh in range(NSA_HPG):
                r = slice(hh * tq, (hh + 1) * tq)
                sb = s[r] + bias
                s_s[g, r] = sb
                row_max = jnp.max(_lane_tile_reduce(sb, jnp.maximum), axis=1, keepdims=True)
                m_prev = m_s[g, r]
                m_new = jnp.maximum(m_prev, jnp.broadcast_to(row_max, (tq, LANES)))
                a_s[g, r] = jnp.exp2(m_prev - m_new)
                m_s[g, r] = m_new
        for g in range(NSA_GROUPS):
            for rb in range(rows // SM_ROWS):
                r = slice(rb * SM_ROWS, (rb + 1) * SM_ROWS)
                p_s[g, r] = jnp.exp2(s_s[g, r] - jnp.tile(m_s[g, r], (1, tk // LANES))).astype(BF16)
        for g in range(NSA_GROUPS):
            data = lax.shift_right_logical(lane, int(math.log2(NSA_DK))) == g
            v_aug = jnp.where(lane == feat_lane[g], 1.0, jnp.where(data, vblk, 0.0)).astype(BF16)
            acc_s[g] = a_s[g] * acc_s[g] + _dot(p_s[g], v_aug)

    @pl.when(step == pl.num_programs(1) - 1)
    def _():
        gates = jax.nn.sigmoid(g_ref[...])
        for h in range(NSA_HEADS):
            g, hh = divmod(h, NSA_HPG)
            r = slice(hh * tq, (hh + 1) * tq)
            sl = slice(h * NSA_DK, (h + 1) * NSA_DK)
            total = acc_s[g, r, feat_lane[g]:feat_lane[g] + 1]
            o = acc_s[g, r, g * NSA_DK:(g + 1) * NSA_DK] / total * gates[:, 3 * h + branch:3 * h + branch + 1]
            if mode == "slc":
                o = (o + ocmp_ref[:, sl] + owin_ref[:, sl]) * _silu(z_ref[:, sl])
            o_ref[:, sl] = o.astype(o_ref.dtype)


def _nsa_flash(proj, mode, sel=None, o_cmp=None, o_win=None, block_any=None):
    s_len = proj.shape[0]
    tq, tk = NSA_TQ, min(NSA_TK, s_len)
    assert tk % LANES == 0 and tk <= 512
    hd = NSA_HEADS * NSA_DK
    gd = NSA_GROUPS * NSA_DK
    k_slab, v_slab, branch = (2, 3, 1) if mode == "slc" else (4, 5, 2)
    nq = s_len // tq
    nsteps = s_len // tk if mode == "slc" else -(-(WINDOW - 1) // tk) + 1
    rows = NSA_HPG * tq
    scratch = [
        pltpu.VMEM((NSA_GROUPS, rows, LANES), BF16),
        pltpu.VMEM((NSA_GROUPS, rows, tk), F32),
        pltpu.VMEM((NSA_GROUPS, rows, tk), BF16),
        pltpu.VMEM((NSA_GROUPS, rows, LANES), F32),
        pltpu.VMEM((NSA_GROUPS, rows, LANES), F32),
        pltpu.VMEM((NSA_GROUPS, rows, LANES), F32),
    ]
    kern = functools.partial(_nsa_flash_kernel, mode=mode, tq=tq, tk=tk, branch=branch)
    out_shape = jax.ShapeDtypeStruct((s_len, hd), BF16 if mode == "slc" else F32)
    if mode == "win":
        def kv_map(slab):
            def index(qi, step):
                lo, hi = _nsa_tile_range(mode, qi, tq, tk)
                return jnp.minimum(lo + step, hi), C_KV // gd + slab
            return index

        q_map = lambda qi, step: (qi, 0)
        return pl.pallas_call(
            kern, out_shape=out_shape, grid=(nq, nsteps),
            in_specs=[
                pl.BlockSpec((tq, hd), lambda qi, step: (qi, C_Q // hd)),
                pl.BlockSpec((tq, LANES), lambda qi, step: (qi, C_G // LANES)),
                pl.BlockSpec((tk, gd), kv_map(k_slab)),
                pl.BlockSpec((tk, gd), kv_map(v_slab)),
            ],
            out_specs=pl.BlockSpec((tq, hd), q_map),
            scratch_shapes=scratch,
            compiler_params=_cparams("parallel", "arbitrary"),
            name="nsa_window",
        )(proj, proj, proj, proj)
    blocks_per_tile = tk // SLC_LEN
    used = block_any[:, 0, :nsteps * blocks_per_tile].reshape(nq, nsteps, blocks_per_tile).max(axis=-1) > 0.5
    fetch = lax.cummax(jnp.where(used, jnp.arange(nsteps, dtype=jnp.int32)[None], 0), axis=1)
    used = used.astype(jnp.int32).reshape(-1)
    fetch = fetch.astype(jnp.int32).reshape(-1)

    def kv_map(slab):
        def index(qi, step, used_ref, fetch_ref):
            _, hi = _nsa_tile_range(mode, qi, tq, tk)
            return jnp.minimum(fetch_ref[qi * nsteps + step], hi), C_KV // gd + slab
        return index

    q_map = lambda qi, step, used_ref, fetch_ref: (qi, 0)
    grid_spec = pltpu.PrefetchScalarGridSpec(
        num_scalar_prefetch=2,
        grid=(nq, nsteps),
        in_specs=[
            pl.BlockSpec((tq, hd), lambda qi, step, u, f: (qi, C_Q // hd)),
            pl.BlockSpec((tq, LANES), lambda qi, step, u, f: (qi, C_G // LANES)),
            pl.BlockSpec((tk, gd), kv_map(k_slab)),
            pl.BlockSpec((tk, gd), kv_map(v_slab)),
            pl.BlockSpec((tq, NSA_GROUPS * SEL_LANES), q_map),
            pl.BlockSpec((tq, hd), q_map),
            pl.BlockSpec((tq, hd), q_map),
            pl.BlockSpec((tq, hd), lambda qi, step, u, f: (qi, C_Z // hd)),
        ],
        out_specs=pl.BlockSpec((tq, hd), q_map),
        scratch_shapes=scratch,
    )
    return pl.pallas_call(
        kern, out_shape=out_shape, grid_spec=grid_spec,
        compiler_params=_cparams("parallel", "arbitrary"),
        name="nsa_selected",
    )(used, fetch, proj, proj, proj, proj, sel, o_cmp, o_win, proj)


def _shift_lerp_kernel(x_ref, xp_ref, g_ref, sc_ref, sh_ref, mu_ref, *o_refs, tm):
    def modulated(x):
        y = x * lax.rsqrt(jnp.mean(x * x, axis=-1, keepdims=True) + EPS) * g_ref[...]
        return y * (1.0 + sc_ref[0, 0:1, :]) + sh_ref[0, 0:1, :]

    h = modulated(x_ref[...])
    before = modulated(xp_ref[...])[7:8, :] * jnp.where(pl.program_id(0) > 0, 1.0, 0.0)
    prev = jnp.where(_iota((tm, 1), 0) == 0, before, pltpu.roll(h, 1, 0))
    xx = prev - h
    for j, o_ref in enumerate(o_refs):
        o_ref[...] = (h + xx * mu_ref[j:j + 1, :]).astype(o_ref.dtype)


def _shift_lerp(x, gain, mod, layer, mu):
    s_len, d = x.shape
    tm = min(ROW_TM, s_len)
    row = pl.BlockSpec((tm, d), lambda i: (i, 0))
    return pl.pallas_call(
        functools.partial(_shift_lerp_kernel, tm=tm),
        out_shape=tuple(jax.ShapeDtypeStruct((s_len, d), BF16) for _ in range(6)),
        grid=(s_len // tm,),
        in_specs=[
            row,
            pl.BlockSpec((8, d), lambda i: (jnp.maximum(i * (tm // 8) - 1, 0), 0)),
            pl.BlockSpec((1, d), lambda i: (0, 0)),
            pl.BlockSpec((1, 8, d), lambda i: (layer, 0, 1)),
            pl.BlockSpec((1, 8, d), lambda i: (layer, 0, 0)),
            pl.BlockSpec((6, d), lambda i: (0, 0)),
        ],
        out_specs=tuple(row for _ in range(6)),
        compiler_params=_cparams("parallel"),
        name="rwkv_shift_lerp",
    )(x, x, gain.reshape(1, d), mod, mod, mu)


def _head_block_diag(value):
    shift = int(math.log2(RWKV_HEAD))
    same = (lax.shift_right_logical(_iota((LANES, LANES), 0), shift)
            == lax.shift_right_logical(_iota((LANES, LANES), 1), shift))
    return jnp.where(same, value, 0.0).astype(BF16)


def _head_sum(x, mat):
    hi = x.astype(BF16)
    lo = (x - hi.astype(F32)).astype(BF16)
    return _dot(hi, mat) + _dot(lo, mat)


def _rwkv_prep_kernel(k_ref, wl_ref, al_ref, w0_ref, a0_ref, kk_ref, ka_ref, lw_o, kk_o, kka_o, k2_o):
    k = k_ref[...]
    lw_o[...] = -math.exp(-0.5) * jax.nn.sigmoid(w0_ref[...] + wl_ref[...])
    a = jax.nn.sigmoid(a0_ref[...] + al_ref[...])
    kkr = k * kk_ref[...]
    ones = _head_block_diag(1.0)
    for j in range(k.shape[1] // LANES):
        sl = slice(j * LANES, (j + 1) * LANES)
        x = kkr[:, sl]
        kk = x / jnp.maximum(jnp.sqrt(_head_sum(x * x, ones)), 1e-12)
        kk_o[:, sl] = kk
        kka_o[:, sl] = kk * a[:, sl]
    k2_o[...] = k * (1.0 + (a - 1.0) * ka_ref[...])


def _rwkv_prep(k, wl, al, w0, a0, k_k, k_a):
    s_len, d = k.shape
    tm = min(ROW_TM, s_len)
    row = pl.BlockSpec((tm, d), lambda i: (i, 0))
    vec = pl.BlockSpec((1, d), lambda i: (0, 0))
    return pl.pallas_call(
        _rwkv_prep_kernel,
        out_shape=tuple(jax.ShapeDtypeStruct((s_len, d), F32) for _ in range(4)),
        grid=(s_len // tm,),
        in_specs=[row, row, row, vec, vec, vec, vec],
        out_specs=(row, row, row, row),
        compiler_params=_cparams("parallel"),
        name="rwkv_prep",
    )(k, wl, al, w0.reshape(1, d), a0.reshape(1, d), k_k.reshape(1, d), k_a.reshape(1, d))


def _rwkv_chunk_kernel(r_ref, lw_ref, k_ref, v_ref, kk_ref, kka_ref, y_ref, s_s, *, c, hb):
    n = RWKV_HEAD

    @pl.when(pl.program_id(1) == 0)
    def _():
        s_s[...] = jnp.zeros(s_s.shape, F32)

    rr = _iota((c, c), 0)
    cc = _iota((c, c), 1)
    lw = lw_ref[...]
    cum = _dot((rr >= cc).astype(F32), lw, HI)
    grow = jnp.exp(-cum)
    decay = jnp.exp(cum)
    total = decay[c - 1:c, :]
    kk = kk_ref[...]
    q_t = kk * jnp.exp(cum - lw)
    p_t = -(kka_ref[...] * grow)
    k_t = k_ref[...] * grow
    r_t = r_ref[...] * decay
    p_end = p_t * total
    k_end = k_t * total
    r2 = _iota((2 * c, 2 * c), 0)
    c2 = _iota((2 * c, 2 * c), 1) & (c - 1)
    gram_mask = (((r2 < c) & (r2 > c2)) | ((r2 >= c) & ((r2 - c) >= c2))).astype(F32)
    eye = (rr == cc).astype(F32)
    same_block = [(lax.shift_right_logical(rr, s) == lax.shift_right_logical(cc, s)).astype(F32)
                  for s in range(3, int(math.log2(c)) + 1)]
    heads = range(hb)
    sls = [slice(i * n, (i + 1) * n) for i in heads]
    qr = [jnp.concatenate([q_t[:, sl], r_t[:, sl]], axis=0).astype(BF16) for sl in sls]
    pk = [jnp.concatenate([p_t[:, sl], k_t[:, sl]], axis=0).astype(BF16) for sl in sls]
    state = [s_s[i] for i in heads]
    from_state = [_dot_nt(qr[i], state[i].astype(BF16)) for i in heads]
    gram = [_dot_nt(qr[i], pk[i]) * gram_mask for i in heads]
    l_qp = [g[:c, :c] for g in gram]
    vb = [v_ref[:, sl].astype(BF16) for sl in sls]
    from_v = [_dot(jnp.concatenate([gram[i][:c, c:], gram[i][c:, c:]], axis=0).astype(BF16), vb[i]) for i in heads]
    power = [l * same_block[0] for l in l_qp]
    inv = [eye + p for p in power]
    for _ in range(2):
        pb = [p.astype(BF16) for p in power]
        power = [_dot(p, p) for p in pb]
        inv = [x + _dot(x.astype(BF16), p.astype(BF16)) for x, p in zip(inv, power)]
    for lvl in range(1, len(same_block)):
        level_mask = same_block[lvl] - same_block[lvl - 1]
        ib = [x.astype(BF16) for x in inv]
        half = [_dot(ib[i], (l_qp[i] * level_mask).astype(BF16)).astype(BF16) for i in heads]
        inv = [inv[i] + _dot(half[i], ib[i]) for i in heads]
    u = [_dot(inv[i].astype(BF16), (from_state[i][:c] + from_v[i][:c]).astype(BF16)) for i in heads]
    ub = [x.astype(BF16) for x in u]
    for i in heads:
        y_ref[:, sls[i]] = from_state[i][c:] + _dot(gram[i][c:, :c].astype(BF16), ub[i]) + from_v[i][c:]
    for i in heads:
        uv = jnp.concatenate([ub[i], vb[i]], axis=0)
        pk_end = jnp.concatenate([p_end[:, sls[i]], k_end[:, sls[i]]], axis=0).astype(BF16)
        s_s[i] = state[i] * total[:, sls[i]] + _dot_tn(uv, pk_end)


def _rwkv_scan(r, lw, k2, v, kk, kka):
    s_len, d = r.shape
    c, hb = RWKV_CHUNK, RWKV_HB
    width = hb * RWKV_HEAD
    blk = pl.BlockSpec((c, width), lambda hg, ci: (ci, hg))
    return pl.pallas_call(
        functools.partial(_rwkv_chunk_kernel, c=c, hb=hb),
        out_shape=jax.ShapeDtypeStruct((s_len, d), F32),
        grid=(d // width, s_len // c),
        in_specs=[blk] * 6,
        out_specs=blk,
        scratch_shapes=[pltpu.VMEM((hb, RWKV_HEAD, RWKV_HEAD), F32)],
        compiler_params=_cparams("parallel", "arbitrary"),
        name="rwkv_chunk_scan",
    )(r, lw, k2, v, kk, kka)


def _rwkv_post_kernel(y_ref, r_ref, k2_ref, v_ref, z_ref, rk_ref, g_ref, b_ref, o_ref):
    avg = _head_block_diag(1.0 / RWKV_HEAD)
    ones = _head_block_diag(1.0)
    for j in range(y_ref.shape[1] // LANES):
        sl = slice(j * LANES, (j + 1) * LANES)
        y = y_ref[:, sl]
        dev = y - _head_sum(y, avg)
        var = _head_sum(dev * dev, avg)
        yn = dev * lax.rsqrt(var + LNX_EPS) * g_ref[:, sl] + b_ref[:, sl]
        bonus = _head_sum(r_ref[:, sl] * k2_ref[:, sl] * rk_ref[:, sl], ones) * v_ref[:, sl]
        o_ref[:, sl] = ((yn + bonus) * _silu(z_ref[:, sl])).astype(o_ref.dtype)


def _rwkv_post(y, r, k2, v, z, r_k, lnx_g, lnx_b):
    s_len, d = y.shape
    tm = min(ROW_TM, s_len)
    row = pl.BlockSpec((tm, d), lambda i: (i, 0))
    vec = pl.BlockSpec((1, d), lambda i: (0, 0))
    return pl.pallas_call(
        _rwkv_post_kernel,
        out_shape=jax.ShapeDtypeStruct((s_len, d), BF16),
        grid=(s_len // tm,),
        in_specs=[row, row, row, row, row, vec, vec, vec],
        out_specs=row,
        compiler_params=_cparams("parallel"),
        name="rwkv_post",
    )(y, r, k2, v, z, r_k.reshape(1, d), lnx_g.reshape(1, d), lnx_b.reshape(1, d))


def _rmsnorm_kernel(x_ref, g_ref, o_ref):
    x = x_ref[...].astype(F32)
    o_ref[...] = (x * lax.rsqrt(jnp.mean(x * x, axis=-1, keepdims=True) + EPS) * g_ref[...]).astype(o_ref.dtype)


def _rmsnorm(x, gain, out_dtype):
    s_len, d = x.shape
    tm = min(ROW_TM, s_len)
    return pl.pallas_call(
        _rmsnorm_kernel,
        out_shape=jax.ShapeDtypeStruct((s_len, d), out_dtype),
        grid=(s_len // tm,),
        in_specs=[pl.BlockSpec((tm, d), lambda i: (i, 0)), pl.BlockSpec((1, d), lambda i: (0, 0))],
        out_specs=pl.BlockSpec((tm, d), lambda i: (i, 0)),
        compiler_params=_cparams("parallel"),
        name="rmsnorm",
    )(x, gain.reshape(1, d))


def _rotate_half_cols(w):
    half = w.shape[1] // 2
    return jnp.concatenate([-w[:, half:], w[:, :half]], axis=1)


def _arrange_w_in(w_in):
    d = w_in.shape[0]
    q, kv, g, z, qa, kva, mlaz = jnp.split(w_in, [1024, 1792, 1840, 2864, 3376, 3952], axis=1)
    ckv, kpe = kva[:, :KV_LORA], kva[:, KV_LORA:]
    pad = jnp.zeros((d, IN0_COLS - C_G - g.shape[1]), w_in.dtype)
    return jnp.concatenate([q, z, mlaz, qa, ckv, kv, kpe, _rotate_half_cols(kpe), g, pad], axis=1).astype(BF16)


def _arrange_w_qb(w_qb):
    w = w_qb.reshape(Q_LORA, MLA_HEADS, QK_NOPE + QK_ROPE)
    nope = w[:, :, :QK_NOPE].reshape(Q_LORA, -1)
    pe = w[:, :, QK_NOPE:]
    half = QK_ROPE // 2
    rot = jnp.concatenate([-pe[:, :, half:], pe[:, :, :half]], axis=2)
    return jnp.concatenate([nope, pe.reshape(Q_LORA, -1), rot.reshape(Q_LORA, -1)], axis=1).astype(BF16)


def _arrange_w_kvb(w_kvb):
    w = w_kvb.reshape(KV_LORA, MLA_HEADS, QK_NOPE + V_HEAD)
    return jnp.concatenate([w[:, :, :QK_NOPE].reshape(KV_LORA, -1), w[:, :, QK_NOPE:].reshape(KV_LORA, -1)],
                           axis=1).astype(BF16)


def _rope_tables(s_len):
    inv = ROPE_THETA ** (-np.arange(0, QK_ROPE, 2, dtype=np.float32) / QK_ROPE)
    ang = np.arange(s_len, dtype=np.float32)[:, None] * inv[None].astype(np.float32)
    cos = np.concatenate([np.cos(ang), np.cos(ang)], axis=1).astype(np.float32)
    sin = np.concatenate([np.sin(ang), np.sin(ang)], axis=1).astype(np.float32)
    return jnp.asarray(cos), jnp.asarray(sin)


def _pad_cols(w, n):
    return jnp.pad(w, ((0, 0), (0, n - w.shape[1])))


def _pad_rows(w, n):
    return jnp.pad(w, ((0, n - w.shape[0]), (0, 0)))


def _attention_layer(x, mod, layer, gain, w_in, w_out, pe_k, w1_k, b1_k, w2_k, pe_v, w1_v, b1_v, w2_v,
                     qa_g, w_qb, kva_g, w_kvb):
    s_len, d = x.shape
    assert s_len % 1024 == 0 and s_len // SLC_LEN <= SEL_LANES
    n_cmp = (s_len - CMP_LEN) // CMP_STRIDE + 1
    n_slc = s_len // SLC_LEN
    proj = _matmul(x, _arrange_w_in(w_in), out_dtype=F32, prologue="rms_mod", gain=gain, mod=mod, layer=layer,
                   tm=IN_PROJ_TM, name="in_proj")
    qfull = _matmul(proj, _arrange_w_qb(w_qb), out_dtype=BF16, a_col=C_QA // Q_LORA, k=Q_LORA, prologue="rms",
                    gain=qa_g, name="mla_q_proj")
    kv = _matmul(proj, _arrange_w_kvb(w_kvb), out_dtype=BF16, a_col=C_CKV // KV_LORA, k=KV_LORA, prologue="rms",
                 gain=kva_g, name="mla_kv_proj")
    cos, sin = _rope_tables(s_len)
    y_mla = _mla_attention(qfull, kv, proj, cos, sin)
    cl = CMP_LEN * NSA_DK
    kcvc = proj[:, C_KV:C_KV + 2 * NSA_GROUPS * NSA_DK].reshape(s_len, 2, NSA_GROUPS, NSA_DK)
    r = kcvc.transpose(1, 2, 0, 3).reshape(2 * NSA_GROUPS, s_len // CMP_STRIDE, CMP_STRIDE * NSA_DK)
    cmp_tokens = _compress_tokens(
        r,
        jnp.stack([w1_k.reshape(cl, -1), w1_v.reshape(cl, -1)]),
        jnp.stack([pe_k.reshape(1, cl), pe_v.reshape(1, cl)]),
        jnp.stack([b1_k.reshape(1, -1), b1_v.reshape(1, -1)]),
        jnp.stack([w2_k, w2_v]),
        n_cmp)
    o_cmp, sel, block_any = _nsa_compressed(proj, cmp_tokens, n_cmp, n_slc)
    o_win = _nsa_flash(proj, "win")
    y_nsa = _nsa_flash(proj, "slc", sel=sel, o_cmp=o_cmp, o_win=o_win, block_any=block_any)
    return _matmul(y_nsa, w_out.astype(BF16), a2=y_mla, out_dtype=F32, epilogue="residual", res=x, mod=mod,
                   layer=layer, name="attn_out_proj")


def _rwkv_layer(x, mod, layer, gain, mu, w_r, w_k, w_v, w_z, w_o, w0, w1, w2, a0, a1, a2, k_k, k_a, r_k,
                lnx_g, lnx_b, final_gain=None):
    xr, xw, xk, xv, xa, xz = _shift_lerp(x, gain, mod, layer, mu)
    r = _matmul(xr, w_r.astype(BF16), out_dtype=F32, name="rwkv_r")
    k = _matmul(xk, w_k.astype(BF16), out_dtype=F32, name="rwkv_k")
    v = _matmul(xv, w_v.astype(BF16), out_dtype=F32, name="rwkv_v")
    z = _matmul(xz, w_z.astype(BF16), out_dtype=F32, name="rwkv_z")
    wl = _matmul(xw, _pad_cols(w1, LORA_PAD).astype(BF16), out_dtype=BF16, epilogue="tanh", name="rwkv_w_lora_a")
    wl = _matmul(wl, _pad_rows(w2, LORA_PAD).astype(BF16), out_dtype=F32, name="rwkv_w_lora_b")
    al = _matmul(xa, _pad_cols(a1, LORA_PAD).astype(BF16), out_dtype=BF16, name="rwkv_a_lora_a")
    al = _matmul(al, _pad_rows(a2, LORA_PAD).astype(BF16), out_dtype=F32, name="rwkv_a_lora_b")
    lw, kk, kka, k2 = _rwkv_prep(k, wl, al, w0, a0, k_k, k_a)
    y = _rwkv_scan(r, lw, k2, v, kk, kka)
    y = _rwkv_post(y, r, k2, v, z, r_k, lnx_g, lnx_b)
    epilogue = "residual" if final_gain is None else "residual_rms"
    return _matmul(y, w_o.astype(BF16), out_dtype=F32, epilogue=epilogue, res=x, mod=mod, layer=layer,
                   final_gain=final_gain, tm=OUT_RMS_TM if final_gain is not None else None, name="rwkv_out_proj")


def kernel(x, c, norm_g, ada_w, ada_b, final_g, a_w_in, a_w_out, nsa_pe_k, nsa_w1_k, nsa_b1_k, nsa_w2_k, nsa_pe_v, nsa_w1_v, nsa_b1_v, nsa_w2_v, mla_qa_g, mla_w_qb, mla_kva_g, mla_w_kvb, r_mu, r_w_r, r_w_k, r_w_v, r_w_z, r_w_o, r_w0, r_w1, r_w2, r_a0, r_a1, r_a2, r_k_k, r_k_a, r_r_k, r_lnx_g, r_lnx_b):
    batch, s_len, d = x.shape
    depth = ada_w.shape[0]
    outs = []
    for b in range(batch):
        mod = _modulation(c[b:b + 1], ada_w, ada_b)
        xb = x[b]
        for i in range(depth):
            j = i // 2
            if i % 2 == 0:
                xb = _attention_layer(xb, mod, i, norm_g[i], a_w_in[j], a_w_out[j],
                                      nsa_pe_k[j], nsa_w1_k[j], nsa_b1_k[j], nsa_w2_k[j],
                                      nsa_pe_v[j], nsa_w1_v[j], nsa_b1_v[j], nsa_w2_v[j],
                                      mla_qa_g[j], mla_w_qb[j], mla_kva_g[j], mla_w_kvb[j])
            else:
                xb = _rwkv_layer(xb, mod, i, norm_g[i], r_mu[j], r_w_r[j], r_w_k[j], r_w_v[j], r_w_z[j], r_w_o[j],
                                 r_w0[j], r_w1[j], r_w2[j], r_a0[j], r_a1[j], r_a2[j], r_k_k[j], r_k_a[j],
                                 r_r_k[j], r_lnx_g[j], r_lnx_b[j], final_gain=final_g if i == depth - 1 else None)
        outs.append(xb if depth % 2 == 0 else _rmsnorm(xb, final_g, x.dtype))
    return jnp.stack(outs)
```

```python
import functools
import math

import jax
import jax.numpy as jnp
import numpy as np
from jax import lax
from jax.experimental import pallas as pl
from jax.experimental.pallas import tpu as pltpu

F32 = jnp.float32
BF16 = jnp.bfloat16
HI = lax.Precision.HIGHEST

EPS = 1e-6
NEG_INF = -1e30
FORCE = 1e9
NSA_HEADS = 16
NSA_GROUPS = 2
NSA_HPG = NSA_HEADS // NSA_GROUPS
NSA_DK = 64
CMP_LEN = 32
CMP_STRIDE = 16
SLC_LEN = 64
SLC_TOPK = 16
WINDOW = 512
MLA_HEADS = 8
Q_LORA = 512
KV_LORA = 512
QK_NOPE = 128
QK_ROPE = 64
V_HEAD = 128
ROPE_THETA = 10000.0
RWKV_HEAD = 64
LORA_PAD = 128
LNX_EPS = 64e-5

LANES = 128
VMEM_LIMIT_BYTES = 48 * 1024 * 1024

MM_TM = 1024
MM_TM_F32 = 512
MM_TN = 1024
IN_PROJ_TM = 1024
OUT_RMS_TM = 512
ROW_TM = 256
MLA_T = 512
NSA_TQ = 128
NSA_TK = 512
NSA_SLOTS = 4
RWKV_CHUNK = 64
RWKV_HB = 32
SEL_LANES = 128
SM_ROWS = 64

LOG2E = 1.0 / math.log(2.0)
M_FLOOR = -1e20

C_Q = 0
C_Z = 1024
C_MLAZ = 2048
C_QA = 3072
C_CKV = 3584
C_KV = 4096
C_KPE = 4864
C_G = 4992
IN0_COLS = 5120


def _cparams(*sem):
    return pltpu.CompilerParams(dimension_semantics=sem, vmem_limit_bytes=VMEM_LIMIT_BYTES)


def _dot(a, b, precision=None):
    return jnp.dot(a, b, preferred_element_type=F32, precision=precision)


def _dot_nt(a, b):
    return lax.dot_general(a, b, (((1,), (1,)), ((), ())), preferred_element_type=F32)


def _dot_tn(a, b):
    return lax.dot_general(a, b, (((0,), (0,)), ((), ())), preferred_element_type=F32)


def _silu(x):
    return x * jax.nn.sigmoid(x)


def _iota(shape, dim):
    return lax.broadcasted_iota(jnp.int32, shape, dim)


def _lane_tile_reduce(x, op):
    parts = [x[:, i * LANES:(i + 1) * LANES] for i in range(x.shape[1] // LANES)]
    while len(parts) > 1:
        parts = [op(parts[i], parts[i + 1]) if i + 1 < len(parts) else parts[i] for i in range(0, len(parts), 2)]
    return parts[0]


def _mod_kernel(c_ref, w_ref, b_ref, o_ref):
    c = c_ref[...]
    sc = jnp.broadcast_to(_silu(c), (8, c.shape[1]))
    o_ref[0] = _dot(sc, w_ref[0], HI) + b_ref[0]


def _modulation(c, ada_w, ada_b):
    depth, d, n = ada_w.shape
    tn = 768
    return pl.pallas_call(
        _mod_kernel,
        out_shape=jax.ShapeDtypeStruct((depth, 8, n), F32),
        grid=(depth, n // tn),
        in_specs=[
            pl.BlockSpec((1, d), lambda i, j: (0, 0)),
            pl.BlockSpec((1, d, tn), lambda i, j: (i, 0, j)),
            pl.BlockSpec((1, 1, tn), lambda i, j: (i, 0, j)),
        ],
        out_specs=pl.BlockSpec((1, 8, tn), lambda i, j: (i, 0, j)),
        compiler_params=_cparams("parallel", "parallel"),
        name="adaln_mod",
    )(c, ada_w, ada_b.reshape(depth, 1, n))


def _mm_kernel(*refs, prologue, epilogue, split_k):
    it = iter(refs)
    a_ref = next(it)
    a2_ref = next(it) if split_k else None
    b_ref = next(it)
    g_ref = next(it) if prologue != "none" else None
    sc_ref, sh_ref = (next(it), next(it)) if prologue == "rms_mod" else (None, None)
    res_ref, gate_ref = (next(it), next(it)) if epilogue.startswith("residual") else (None, None)
    fg_ref = next(it) if epilogue == "residual_rms" else None
    o_ref = next(it)
    if prologue == "none":
        a = a_ref[...]
    else:
        a_s = next(it)

        @pl.when(pl.program_id(1) == 0)
        def _():
            x = a_ref[...].astype(F32)
            y = x * lax.rsqrt(jnp.mean(x * x, axis=-1, keepdims=True) + EPS) * g_ref[...]
            if prologue == "rms_mod":
                y = y * (1.0 + sc_ref[0, 0:1, :]) + sh_ref[0, 0:1, :]
            a_s[...] = y.astype(BF16)

        a = a_s[...]
    if split_k:
        k1 = a.shape[1]
        acc = _dot(a, b_ref[0:k1, :]) + _dot(a2_ref[...], b_ref[k1:, :])
    else:
        acc = _dot(a, b_ref[...])
    if epilogue == "tanh":
        acc = jnp.tanh(acc)
    elif epilogue.startswith("residual"):
        acc = res_ref[...] + gate_ref[0, 0:1, :] * acc
        if epilogue == "residual_rms":
            acc = acc * lax.rsqrt(jnp.mean(acc * acc, axis=-1, keepdims=True) + EPS) * fg_ref[...]
    o_ref[...] = acc.astype(o_ref.dtype)


def _matmul(a, b, *, out_dtype, a2=None, a_col=0, k=None, prologue="none", gain=None, mod=None, layer=0,
            epilogue="none", res=None, final_gain=None, tm=None, tn=None, name="matmul"):
    m = a.shape[0]
    n = b.shape[1]
    k = (b.shape[0] if a2 is None else a.shape[1]) if k is None else k
    d_mod = None if mod is None else mod.shape[2] // 3
    tm = min(tm or (MM_TM if prologue == "none" else MM_TM_F32), m)
    tn = n if epilogue == "residual_rms" else min(tn or MM_TN, n)
    in_specs = [pl.BlockSpec((tm, k), lambda i, j: (i, a_col))]
    args = [a]
    if a2 is not None:
        in_specs.append(pl.BlockSpec((tm, a2.shape[1]), lambda i, j: (i, 0)))
        args.append(a2)
    in_specs.append(pl.BlockSpec((b.shape[0], tn), lambda i, j: (0, j)))
    args.append(b)
    scratch = []
    if prologue != "none":
        in_specs.append(pl.BlockSpec((1, k), lambda i, j: (0, 0)))
        args.append(gain.reshape(1, k).astype(F32))
        scratch.append(pltpu.VMEM((tm, k), BF16))
    if prologue == "rms_mod":
        in_specs += [pl.BlockSpec((1, 8, k), lambda i, j: (layer, 0, 1)), pl.BlockSpec((1, 8, k), lambda i, j: (layer, 0, 0))]
        args += [mod, mod]
    if epilogue.startswith("residual"):
        in_specs += [pl.BlockSpec((tm, tn), lambda i, j: (i, j)),
                     pl.BlockSpec((1, 8, tn), lambda i, j: (layer, 0, 2 * (d_mod // tn) + j))]
        args += [res, mod]
    if epilogue == "residual_rms":
        in_specs.append(pl.BlockSpec((1, n), lambda i, j: (0, 0)))
        args.append(final_gain.reshape(1, n))
    return pl.pallas_call(
        functools.partial(_mm_kernel, prologue=prologue, epilogue=epilogue, split_k=a2 is not None),
        out_shape=jax.ShapeDtypeStruct((m, n), out_dtype),
        grid=(m // tm, n // tn),
        in_specs=in_specs,
        out_specs=pl.BlockSpec((tm, tn), lambda i, j: (i, j)),
        scratch_shapes=scratch,
        compiler_params=_cparams("parallel", "arbitrary"),
        name=name,
    )(*args)


def _mla_attn_kernel(qn_ref, qp_ref, qpr_ref, cq_ref, sq_ref, k_ref, v_ref, kpe_ref, ck_ref, sk_ref, z_ref,
                     o_ref, q_s, kc_s, s_s, p_s, a_s, m_s, l_s, acc_s, *, t, scale):
    qi = pl.program_id(0)
    ki = pl.program_id(1)
    dq = QK_NOPE + QK_ROPE

    @pl.when(ki == 0)
    def _():
        qn = (qn_ref[...].astype(F32) * (scale * LOG2E)).astype(BF16)
        cos = cq_ref[...] * (scale * LOG2E)
        sin = sq_ref[...] * (scale * LOG2E)
        for h in range(MLA_HEADS):
            pe = slice(h * QK_ROPE, (h + 1) * QK_ROPE)
            qp = qp_ref[:, pe].astype(F32) * cos + qpr_ref[:, pe].astype(F32) * sin
            q_s[:, h * dq:h * dq + QK_NOPE] = qn[:, h * QK_NOPE:(h + 1) * QK_NOPE]
            q_s[:, h * dq + QK_NOPE:(h + 1) * dq] = qp.astype(BF16)
        m_s[...] = jnp.full(m_s.shape, M_FLOOR, F32)
        l_s[...] = jnp.zeros(l_s.shape, F32)
        acc_s[...] = jnp.zeros(acc_s.shape, F32)

    def step(diagonal):
        kpe = kpe_ref[...]
        kp = (kpe[:, :QK_ROPE] * ck_ref[...] + kpe[:, QK_ROPE:] * sk_ref[...]).astype(BF16)
        for h in range(MLA_HEADS):
            kc_s[h, :, 0:QK_NOPE] = k_ref[:, h * QK_NOPE:(h + 1) * QK_NOPE]
            kc_s[h, :, QK_NOPE:dq] = kp
        for h in range(MLA_HEADS):
            s = _dot_nt(q_s[:, h * dq:(h + 1) * dq], kc_s[h])
            if diagonal:
                s = jnp.where(_iota((t, 1), 0) >= _iota((1, t), 1), s, NEG_INF)
            s_s[h] = s
            row_max = jnp.max(_lane_tile_reduce(s, jnp.maximum), axis=1, keepdims=True)
            m_prev = m_s[h]
            m_new = jnp.maximum(m_prev, jnp.broadcast_to(row_max, (t, LANES)))
            a_s[h] = jnp.exp2(m_prev - m_new)
            m_s[h] = m_new
        for h in range(MLA_HEADS):
            for rb in range(t // SM_ROWS):
                r = slice(rb * SM_ROWS, (rb + 1) * SM_ROWS)
                p = jnp.exp2(s_s[h, r] - jnp.tile(m_s[h, r], (1, t // LANES)))
                p_s[h, r] = p.astype(BF16)
                row_sum = jnp.sum(_lane_tile_reduce(p, jnp.add), axis=1, keepdims=True)
                l_s[h, r] = a_s[h, r] * l_s[h, r] + jnp.broadcast_to(row_sum, (SM_ROWS, LANES))
        for h in range(MLA_HEADS):
            acc_s[h] = a_s[h] * acc_s[h] + _dot(p_s[h], v_ref[:, h * V_HEAD:(h + 1) * V_HEAD])

    @pl.when(ki < qi)
    def _():
        step(False)

    @pl.when(ki == qi)
    def _():
        step(True)
        for h in range(MLA_HEADS):
            sl = slice(h * V_HEAD, (h + 1) * V_HEAD)
            o_ref[:, sl] = (acc_s[h] / l_s[h] * _silu(z_ref[:, sl])).astype(o_ref.dtype)


def _mla_attention(qfull, kv, proj, cos, sin):
    s_len = qfull.shape[0]
    t = min(MLA_T, s_len)
    n = s_len // t
    hn = MLA_HEADS * QK_NOPE
    hp = MLA_HEADS * QK_ROPE
    hv = MLA_HEADS * V_HEAD
    kv_idx = lambda qi, ki: (jnp.minimum(ki, qi), 0)
    return pl.pallas_call(
        functools.partial(_mla_attn_kernel, t=t, scale=(QK_NOPE + QK_ROPE) ** -0.5),
        out_shape=jax.ShapeDtypeStruct((s_len, hv), BF16),
        grid=(n, n),
        in_specs=[
            pl.BlockSpec((t, hn), lambda qi, ki: (qi, 0)),
            pl.BlockSpec((t, hp), lambda qi, ki: (qi, hn // hp)),
            pl.BlockSpec((t, hp), lambda qi, ki: (qi, hn // hp + 1)),
            pl.BlockSpec((t, QK_ROPE), lambda qi, ki: (qi, 0)),
            pl.BlockSpec((t, QK_ROPE), lambda qi, ki: (qi, 0)),
            pl.BlockSpec((t, hn), kv_idx),
            pl.BlockSpec((t, hv), lambda qi, ki: (jnp.minimum(ki, qi), 1)),
            pl.BlockSpec((t, 2 * QK_ROPE), lambda qi, ki: (jnp.minimum(ki, qi), C_KPE // (2 * QK_ROPE))),
            pl.BlockSpec((t, QK_ROPE), kv_idx),
            pl.BlockSpec((t, QK_ROPE), kv_idx),
            pl.BlockSpec((t, hv), lambda qi, ki: (qi, C_MLAZ // hv)),
        ],
        out_specs=pl.BlockSpec((t, hv), lambda qi, ki: (qi, 0)),
        scratch_shapes=[
            pltpu.VMEM((t, hn + hp), BF16),
            pltpu.VMEM((MLA_HEADS, t, QK_NOPE + QK_ROPE), BF16),
            pltpu.VMEM((MLA_HEADS, t, t), F32),
            pltpu.VMEM((MLA_HEADS, t, t), BF16),
            pltpu.VMEM((MLA_HEADS, t, LANES), F32),
            pltpu.VMEM((MLA_HEADS, t, LANES), F32),
            pltpu.VMEM((MLA_HEADS, t, LANES), F32),
            pltpu.VMEM((MLA_HEADS, t, V_HEAD), F32),
        ],
        compiler_params=_cparams("parallel", "arbitrary"),
        name="mla_attention",
    )(qfull, qfull, qfull, cos, sin, kv, kv, proj, cos, sin, proj)


def _cmp_tokens_kernel(r_ref, w1_ref, pe_ref, b1_ref, w2_ref, o_ref, *, n_cmp):
    r = r_ref[0]
    w1 = w1_ref[0]
    half = w1.shape[0] // 2
    first = _dot(r, w1[:half], HI)
    second = _dot(r, w1[half:], HI)
    rows = r.shape[0]
    second = pltpu.roll(second, rows - 1, 0)
    hid = first + second + _dot(pe_ref[0], w1, HI) + b1_ref[0]
    out = _dot(_silu(hid), w2_ref[0], HI)
    o_ref[0] = jnp.where(_iota((rows, 1), 0) < n_cmp, out, 0.0)


def _compress_tokens(r, w1, pe, b1, w2, n_cmp):
    four, rows, width = r.shape
    hid = w1.shape[2]
    dk = w2.shape[2]
    return pl.pallas_call(
        functools.partial(_cmp_tokens_kernel, n_cmp=n_cmp),
        out_shape=jax.ShapeDtypeStruct((four, rows, dk), F32),
        grid=(four,),
        in_specs=[
            pl.BlockSpec((1, rows, width), lambda i: (i, 0, 0)),
            pl.BlockSpec((1, 2 * width, hid), lambda i: (i // NSA_GROUPS, 0, 0)),
            pl.BlockSpec((1, 1, 2 * width), lambda i: (i // NSA_GROUPS, 0, 0)),
            pl.BlockSpec((1, 1, hid), lambda i: (i // NSA_GROUPS, 0, 0)),
            pl.BlockSpec((1, hid, dk), lambda i: (i // NSA_GROUPS, 0, 0)),
        ],
        out_specs=pl.BlockSpec((1, rows, dk), lambda i: (i, 0, 0)),
        compiler_params=_cparams("parallel"),
        name="nsa_compress_tokens",
    )(r, w1, pe, b1, w2)


def _alibi_slope(h):
    return 2.0 ** (-8.0 * (h + 1) / NSA_HEADS)


def _bf16_parts(x, n=3):
    parts = []
    for _ in range(n):
        hi = float(np.asarray(x, dtype=BF16).astype(np.float32))
        parts.append(hi)
        x = x - hi
    return parts


def _nsa_cmp_kernel(q_ref, g_ref, cmp_ref, o_ref, sel_ref, any_ref, qa_s, ka_s, va_s, s_s, m_s, ehi_s, elo_s,
                    *, tq, n_cmp, n_slc):
    qi = pl.program_id(0)
    n_pad = cmp_ref.shape[1]
    t_col = qi * tq + _iota((tq, 1), 0)
    n_row = _iota((1, n_pad), 1)
    cmp_end = n_row * CMP_STRIDE + (CMP_LEN - 1)
    bias = jnp.where((cmp_end <= t_col) & (n_row < n_cmp), 0.0, NEG_INF)
    gates = jax.nn.sigmoid(g_ref[...])
    lane = _iota((n_pad, NSA_DK), 1)
    pos = _iota((n_pad, NSA_DK), 0) * CMP_STRIDE + (CMP_LEN - 1)
    upper = (pos & ~255).astype(F32)
    lower = (pos & 255).astype(F32)
    feat = jnp.where(lane < 3, upper, jnp.where(lane < 6, lower, 0.0)).astype(BF16)
    ones_col = jnp.where(lane == 0, 1.0, 0.0).astype(BF16)
    for g in range(NSA_GROUPS):
        ka_s[g, :, 0:NSA_DK] = cmp_ref[g].astype(BF16)
        ka_s[g, :, NSA_DK:2 * NSA_DK] = feat
        va_s[g, :, 0:NSA_DK] = cmp_ref[NSA_GROUPS + g].astype(BF16)
        va_s[g, :, NSA_DK:2 * NSA_DK] = ones_col
    qlane = _iota((1, NSA_DK), 1)
    for h in range(NSA_HEADS):
        qa_s[h, :, 0:NSA_DK] = (q_ref[:, h * NSA_DK:(h + 1) * NSA_DK] * (NSA_DK ** -0.5 * LOG2E)).astype(BF16)
        qfeat = jnp.zeros((1, NSA_DK), F32)
        for i, part in enumerate(2 * _bf16_parts(LOG2E * _alibi_slope(h))):
            qfeat = jnp.where(qlane == i, part, qfeat)
        qa_s[h, :, NSA_DK:2 * NSA_DK] = jnp.broadcast_to(qfeat, (tq, NSA_DK)).astype(BF16)
    n_c = _iota((n_pad, SEL_LANES), 0) * CMP_STRIDE
    j_c = _iota((n_pad, SEL_LANES), 1) * SLC_LEN
    overlap = ((n_c < j_c + SLC_LEN) & (n_c + CMP_LEN > j_c)).astype(BF16)
    j_row = _iota((1, SEL_LANES), 1)
    cur = lax.shift_right_logical(t_col, int(math.log2(SLC_LEN)))
    forced = (j_row == 0) | (j_row == cur) | (j_row == cur - 1)
    causal = (j_row * SLC_LEN <= t_col) & (j_row < n_slc)
    j_t = _iota((SEL_LANES, tq), 0)
    for h in range(NSA_HEADS):
        s = _dot_nt(qa_s[h], ka_s[h // NSA_HPG]) + bias
        s_s[h] = s
        row_max = jnp.max(_lane_tile_reduce(s, jnp.maximum), axis=1, keepdims=True)
        m_s[h] = jnp.maximum(jnp.broadcast_to(row_max, (tq, LANES)), M_FLOOR)
    for h in range(NSA_HEADS):
        e = jnp.exp2(s_s[h] - jnp.tile(m_s[h], (1, n_pad // LANES)))
        e_hi = e.astype(BF16)
        ehi_s[h] = e_hi
        elo_s[h] = (e - e_hi.astype(F32)).astype(BF16)
    for g in range(NSA_GROUPS):
        imp = jnp.zeros((tq, SEL_LANES), F32)
        for hh in range(NSA_HPG):
            h = g * NSA_HPG + hh
            sl = slice(h * NSA_DK, (h + 1) * NSA_DK)
            pv = _dot(ehi_s[h], va_s[g]) + _dot(elo_s[h], va_s[g])
            den = pv[:, NSA_DK:NSA_DK + 1]
            inv_den = 1.0 / jnp.where(den > 0.0, den, 1.0)
            o_ref[:, sl] = pv[:, 0:NSA_DK] * (inv_den * gates[:, 3 * h:3 * h + 1])
            imp = imp + (_dot(ehi_s[h], overlap) + _dot(elo_s[h], overlap)) * inv_den
        imp = jnp.where(forced, FORCE, imp)
        imp = jnp.where(causal, imp, jnp.where(j_row < n_slc, NEG_INF, -3e38))
        imp_t = imp.T
        sel_t = jnp.zeros((SEL_LANES, tq), F32)
        for _ in range(min(SLC_TOPK, n_slc)):
            best = jnp.max(imp_t, axis=0, keepdims=True)
            first = jnp.min(jnp.where(imp_t == best, j_t, SEL_LANES), axis=0, keepdims=True)
            hit = j_t == first
            sel_t = jnp.where(hit, 1.0, sel_t)
            imp_t = jnp.where(hit, -3.4e38, imp_t)
        sel = sel_t.T
        sel_ref[:, g * SEL_LANES:(g + 1) * SEL_LANES] = sel.astype(sel_ref.dtype)
        tile_any = jnp.max(sel, axis=0, keepdims=True)
        picked_any = tile_any if g == 0 else jnp.maximum(picked_any, tile_any)
    any_ref[0] = jnp.broadcast_to(picked_any, (8, SEL_LANES))


def _nsa_compressed(proj, cmp_tokens, n_cmp, n_slc):
    s_len = proj.shape[0]
    tq = NSA_TQ
    hd = NSA_HEADS * NSA_DK
    return pl.pallas_call(
        functools.partial(_nsa_cmp_kernel, tq=tq, n_cmp=n_cmp, n_slc=n_slc),
        out_shape=(jax.ShapeDtypeStruct((s_len, hd), F32),
                   jax.ShapeDtypeStruct((s_len, NSA_GROUPS * SEL_LANES), BF16),
                   jax.ShapeDtypeStruct((s_len // tq, 8, SEL_LANES), F32)),
        grid=(s_len // tq,),
        in_specs=[
            pl.BlockSpec((tq, hd), lambda i: (i, C_Q // hd)),
            pl.BlockSpec((tq, LANES), lambda i: (i, C_G // LANES)),
            pl.BlockSpec(cmp_tokens.shape, lambda i: (0, 0, 0)),
        ],
        out_specs=(pl.BlockSpec((tq, hd), lambda i: (i, 0)),
                   pl.BlockSpec((tq, NSA_GROUPS * SEL_LANES), lambda i: (i, 0)),
                   pl.BlockSpec((1, 8, SEL_LANES), lambda i: (i, 0, 0))),
        scratch_shapes=[
            pltpu.VMEM((NSA_HEADS, tq, 2 * NSA_DK), BF16),
            pltpu.VMEM((NSA_GROUPS, cmp_tokens.shape[1], 2 * NSA_DK), BF16),
            pltpu.VMEM((NSA_GROUPS, cmp_tokens.shape[1], 2 * NSA_DK), BF16),
            pltpu.VMEM((NSA_HEADS, tq, cmp_tokens.shape[1]), F32),
            pltpu.VMEM((NSA_HEADS, tq, LANES), F32),
            pltpu.VMEM((NSA_HEADS, tq, cmp_tokens.shape[1]), BF16),
            pltpu.VMEM((NSA_HEADS, tq, cmp_tokens.shape[1]), BF16),
        ],
        compiler_params=_cparams("parallel"),
        name="nsa_compressed_topk",
    )(proj, proj, cmp_tokens)


def _nsa_tile_range(mode, qi, tq, tk):
    q0 = qi * tq
    last = lax.div(q0 + tq - 1, tk)
    if mode == "slc":
        return jnp.zeros_like(last), last
    return lax.div(jnp.maximum(q0 - (WINDOW - 1), 0), tk), last


def _nsa_flash_kernel(*refs, mode, tq, tk, branch, slots):
    if mode == "slc":
        tiles_ref, count_ref = refs[:2]
        q_ref, g_ref = refs[2:4]
        k_refs = refs[4:4 + slots]
        v_refs = refs[4 + slots:4 + 2 * slots]
        sel_ref, ocmp_ref, owin_ref, z_ref, o_ref, q_s, s_s, p_s, a_s, m_s, acc_s = refs[4 + 2 * slots:]
    else:
        q_ref, g_ref, k_ref, v_ref, o_ref, q_s, s_s, p_s, a_s, m_s, acc_s = refs
    qi = pl.program_id(0)
    step = pl.program_id(1)
    rows = NSA_HPG * tq
    feat_lane = [NSA_DK * (1 - g) for g in range(NSA_GROUPS)]

    @pl.when(step == 0)
    def _():
        lane = _iota((1, NSA_DK), 1)
        for h in range(NSA_HEADS):
            g, hh = divmod(h, NSA_HPG)
            r = slice(hh * tq, (hh + 1) * tq)
            q = q_ref[:, h * NSA_DK:(h + 1) * NSA_DK] * (NSA_DK ** -0.5 * LOG2E)
            q_s[g, r, g * NSA_DK:(g + 1) * NSA_DK] = q.astype(BF16)
            feat = jnp.zeros((1, NSA_DK), F32)
            for i, part in enumerate(3 * _bf16_parts(LOG2E * _alibi_slope(h))):
                feat = jnp.where(lane == i, part, feat)
            q_s[g, r, feat_lane[g]:feat_lane[g] + NSA_DK] = jnp.broadcast_to(feat, (tq, NSA_DK)).astype(BF16)
        m_s[...] = jnp.full(m_s.shape, M_FLOOR, F32)
        acc_s[...] = jnp.zeros(acc_s.shape, F32)

    def process(k_ref, v_ref, kt):
        k0 = kt * tk
        delta = (qi * tq + _iota((tq, 1), 0)) - (k0 + _iota((1, tk), 1))
        if mode == "slc":
            blk = lax.div(k0, SLC_LEN) + lax.shift_right_logical(_iota((SEL_LANES, tk), 1), int(math.log2(SLC_LEN)))
            expand = (_iota((SEL_LANES, tk), 0) == blk).astype(BF16)
        else:
            bias = jnp.where((delta >= 0) & (delta < WINDOW), 0.0, NEG_INF)
        jj = _iota((tk, LANES), 0)
        lane = _iota((tk, LANES), 1)
        tile_off = (k0 - qi * tq).astype(F32)
        upper = jnp.where(jj >= 256, 256.0, 0.0)
        lower = (jj & 255).astype(F32)
        kblk = k_ref[...]
        vblk = v_ref[...]
        for g in range(NSA_GROUPS):
            if mode == "slc":
                picked = _dot(sel_ref[:, g * SEL_LANES:(g + 1) * SEL_LANES], expand)
                bias = jnp.where((picked > 0.5) & (delta >= 0), 0.0, NEG_INF)
            f = lane - feat_lane[g]
            data = lax.shift_right_logical(lane, int(math.log2(NSA_DK))) == g
            k_aug = jnp.where(data, kblk, 0.0)
            k_aug = jnp.where((f >= 0) & (f < 3), tile_off, k_aug)
            k_aug = jnp.where((f >= 3) & (f < 6), upper, k_aug)
            k_aug = jnp.where((f >= 6) & (f < 9), lower, k_aug)
            s = _dot_nt(q_s[g], k_aug.astype(BF16))
            for hh in range(NSA_HPG):
                r = slice(hh * tq, (hh + 1) * tq)
                sb = s[r] + bias
                s_s[g, r] = sb
                row_max = jnp.max(_lane_tile_reduce(sb, jnp.maximum), axis=1, keepdims=True)
                m_prev = m_s[g, r]
                m_new = jnp.maximum(m_prev, jnp.broadcast_to(row_max, (tq, LANES)))
                a_s[g, r] = jnp.exp2(m_prev - m_new)
                m_s[g, r] = m_new
        for g in range(NSA_GROUPS):
            for rb in range(rows // SM_ROWS):
                r = slice(rb * SM_ROWS, (rb + 1) * SM_ROWS)
                p_s[g, r] = jnp.exp2(s_s[g, r] - jnp.tile(m_s[g, r], (1, tk // LANES))).astype(BF16)
        for g in range(NSA_GROUPS):
            data = lax.shift_right_logical(lane, int(math.log2(NSA_DK))) == g
            v_aug = jnp.where(lane == feat_lane[g], 1.0, jnp.where(data, vblk, 0.0)).astype(BF16)
            acc_s[g] = a_s[g] * acc_s[g] + _dot(p_s[g], v_aug)

    if mode == "slc":
        per_tile = pl.num_programs(1) * slots
        for slot in range(slots):
            entry = step * slots + slot

            @pl.when(entry < count_ref[qi])
            def _(slot=slot, entry=entry):
                process(k_refs[slot], v_refs[slot], tiles_ref[qi * per_tile + entry])
    else:
        lo, hi = _nsa_tile_range(mode, qi, tq, tk)

        @pl.when(lo + step <= hi)
        def _():
            process(k_ref, v_ref, lo + step)

    @pl.when(step == pl.num_programs(1) - 1)
    def _():
        gates = jax.nn.sigmoid(g_ref[...])
        for h in range(NSA_HEADS):
            g, hh = divmod(h, NSA_HPG)
            r = slice(hh * tq, (hh + 1) * tq)
            sl = slice(h * NSA_DK, (h + 1) * NSA_DK)
            total = acc_s[g, r, feat_lane[g]:feat_lane[g] + 1]
            o = acc_s[g, r, g * NSA_DK:(g + 1) * NSA_DK] / total * gates[:, 3 * h + branch:3 * h + branch + 1]
            if mode == "slc":
                o = (o + ocmp_ref[:, sl] + owin_ref[:, sl]) * _silu(z_ref[:, sl])
            o_ref[:, sl] = o.astype(o_ref.dtype)


def _nsa_flash(proj, mode, sel=None, o_cmp=None, o_win=None, block_any=None):
    s_len = proj.shape[0]
    tq, tk = NSA_TQ, min(NSA_TK, s_len)
    assert tk % LANES == 0 and tk <= 512
    hd = NSA_HEADS * NSA_DK
    gd = NSA_GROUPS * NSA_DK
    k_slab, v_slab, branch = (2, 3, 1) if mode == "slc" else (4, 5, 2)
    nq = s_len // tq
    nsteps = s_len // tk if mode == "slc" else -(-(WINDOW - 1) // tk) + 1
    rows = NSA_HPG * tq
    scratch = [
        pltpu.VMEM((NSA_GROUPS, rows, LANES), BF16),
        pltpu.VMEM((NSA_GROUPS, rows, tk), F32),
        pltpu.VMEM((NSA_GROUPS, rows, tk), BF16),
        pltpu.VMEM((NSA_GROUPS, rows, LANES), F32),
        pltpu.VMEM((NSA_GROUPS, rows, LANES), F32),
        pltpu.VMEM((NSA_GROUPS, rows, LANES), F32),
    ]
    out_shape = jax.ShapeDtypeStruct((s_len, hd), BF16 if mode == "slc" else F32)
    if mode == "win":
        def kv_map(slab):
            def index(qi, step):
                lo, hi = _nsa_tile_range(mode, qi, tq, tk)
                return jnp.minimum(lo + step, hi), C_KV // gd + slab
            return index

        q_map = lambda qi, step: (qi, 0)
        return pl.pallas_call(
            functools.partial(_nsa_flash_kernel, mode=mode, tq=tq, tk=tk, branch=branch, slots=1),
            out_shape=out_shape, grid=(nq, nsteps),
            in_specs=[
                pl.BlockSpec((tq, hd), lambda qi, step: (qi, C_Q // hd)),
                pl.BlockSpec((tq, LANES), lambda qi, step: (qi, C_G // LANES)),
                pl.BlockSpec((tk, gd), kv_map(k_slab)),
                pl.BlockSpec((tk, gd), kv_map(v_slab)),
            ],
            out_specs=pl.BlockSpec((tq, hd), q_map),
            scratch_shapes=scratch,
            compiler_params=_cparams("parallel", "arbitrary"),
            name="nsa_window",
        )(proj, proj, proj, proj)
    ntiles = s_len // tk
    slots = min(NSA_SLOTS, ntiles)
    blocks_per_tile = tk // SLC_LEN
    used = block_any[:, 0, :ntiles * blocks_per_tile].reshape(nq, ntiles, blocks_per_tile).max(axis=-1) > 0.5
    tile_ids = jnp.arange(ntiles, dtype=jnp.int32)
    causal = tile_ids[None, :] * tk <= (jnp.arange(nq, dtype=jnp.int32)[:, None] * tq + tq - 1)
    visit = used & causal
    count = jnp.maximum(visit.sum(axis=1).astype(jnp.int32), 1)
    order = jnp.argsort(jnp.where(visit, tile_ids[None, :], ntiles + tile_ids[None, :]), axis=1).astype(jnp.int32)
    tiles = jnp.take_along_axis(order, jnp.minimum(tile_ids[None, :], count[:, None] - 1), axis=1).reshape(-1)

    def kv_map(slab, slot):
        def index(qi, step, tiles_ref, count_ref):
            return tiles_ref[qi * ntiles + step * slots + slot], C_KV // gd + slab
        return index

    q_map = lambda qi, step, t, c: (qi, 0)
    grid_spec = pltpu.PrefetchScalarGridSpec(
        num_scalar_prefetch=2,
        grid=(nq, ntiles // slots),
        in_specs=[
            pl.BlockSpec((tq, hd), lambda qi, step, t, c: (qi, C_Q // hd)),
            pl.BlockSpec((tq, LANES), lambda qi, step, t, c: (qi, C_G // LANES)),
            *[pl.BlockSpec((tk, gd), kv_map(k_slab, slot)) for slot in range(slots)],
            *[pl.BlockSpec((tk, gd), kv_map(v_slab, slot)) for slot in range(slots)],
            pl.BlockSpec((tq, NSA_GROUPS * SEL_LANES), q_map),
            pl.BlockSpec((tq, hd), q_map),
            pl.BlockSpec((tq, hd), q_map),
            pl.BlockSpec((tq, hd), lambda qi, step, t, c: (qi, C_Z // hd)),
        ],
        out_specs=pl.BlockSpec((tq, hd), q_map),
        scratch_shapes=scratch,
    )
    return pl.pallas_call(
        functools.partial(_nsa_flash_kernel, mode=mode, tq=tq, tk=tk, branch=branch, slots=slots),
        out_shape=out_shape, grid_spec=grid_spec,
        compiler_params=_cparams("parallel", "arbitrary"),
        name="nsa_selected",
    )(tiles, count, proj, proj, *([proj] * (2 * slots)), sel, o_cmp, o_win, proj)


def _shift_lerp_kernel(x_ref, xp_ref, g_ref, sc_ref, sh_ref, mu_ref, *o_refs, tm):
    def modulated(x):
        y = x * lax.rsqrt(jnp.mean(x * x, axis=-1, keepdims=True) + EPS) * g_ref[...]
        return y * (1.0 + sc_ref[0, 0:1, :]) + sh_ref[0, 0:1, :]

    h = modulated(x_ref[...])
    before = modulated(xp_ref[...])[7:8, :] * jnp.where(pl.program_id(0) > 0, 1.0, 0.0)
    prev = jnp.where(_iota((tm, 1), 0) == 0, before, pltpu.roll(h, 1, 0))
    xx = prev - h
    for j, o_ref in enumerate(o_refs):
        o_ref[...] = (h + xx * mu_ref[j:j + 1, :]).astype(o_ref.dtype)


def _shift_lerp(x, gain, mod, layer, mu):
    s_len, d = x.shape
    tm = min(ROW_TM, s_len)
    row = pl.BlockSpec((tm, d), lambda i: (i, 0))
    return pl.pallas_call(
        functools.partial(_shift_lerp_kernel, tm=tm),
        out_shape=tuple(jax.ShapeDtypeStruct((s_len, d), BF16) for _ in range(6)),
        grid=(s_len // tm,),
        in_specs=[
            row,
            pl.BlockSpec((8, d), lambda i: (jnp.maximum(i * (tm // 8) - 1, 0), 0)),
            pl.BlockSpec((1, d), lambda i: (0, 0)),
            pl.BlockSpec((1, 8, d), lambda i: (layer, 0, 1)),
            pl.BlockSpec((1, 8, d), lambda i: (layer, 0, 0)),
            pl.BlockSpec((6, d), lambda i: (0, 0)),
        ],
        out_specs=tuple(row for _ in range(6)),
        compiler_params=_cparams("parallel"),
        name="rwkv_shift_lerp",
    )(x, x, gain.reshape(1, d), mod, mod, mu)


def _head_block_diag(value):
    shift = int(math.log2(RWKV_HEAD))
    same = (lax.shift_right_logical(_iota((LANES, LANES), 0), shift)
            == lax.shift_right_logical(_iota((LANES, LANES), 1), shift))
    return jnp.where(same, value, 0.0).astype(BF16)


def _head_sum(x, mat):
    hi = x.astype(BF16)
    lo = (x - hi.astype(F32)).astype(BF16)
    return _dot(hi, mat) + _dot(lo, mat)


def _rwkv_prep_kernel(k_ref, wa_ref, aa_ref, w2_ref, a2_ref, w0_ref, a0_ref, kk_ref, ka_ref, lw_o, kk_o, kka_o, k2_o):
    k = k_ref[...].astype(F32)
    lw_o[...] = -math.exp(-0.5) * jax.nn.sigmoid(w0_ref[...] + _dot(wa_ref[...], w2_ref[...]))
    a = jax.nn.sigmoid(a0_ref[...] + _dot(aa_ref[...], a2_ref[...]))
    kkr = k * kk_ref[...]
    ones = _head_block_diag(1.0)
    for j in range(k.shape[1] // LANES):
        sl = slice(j * LANES, (j + 1) * LANES)
        x = kkr[:, sl]
        kk = x / jnp.maximum(jnp.sqrt(_head_sum(x * x, ones)), 1e-12)
        kk_o[:, sl] = kk.astype(kk_o.dtype)
        kka_o[:, sl] = (kk * a[:, sl]).astype(kka_o.dtype)
    k2_o[...] = (k * (1.0 + (a - 1.0) * ka_ref[...])).astype(k2_o.dtype)


def _rwkv_prep(k, w_lora, a_lora, w2, a2, w0, a0, k_k, k_a):
    s_len, d = k.shape
    tm = min(ROW_TM, s_len)
    row = pl.BlockSpec((tm, d), lambda i: (i, 0))
    low = pl.BlockSpec((tm, LORA_PAD), lambda i: (i, 0))
    up = pl.BlockSpec((LORA_PAD, d), lambda i: (0, 0))
    vec = pl.BlockSpec((1, d), lambda i: (0, 0))
    return pl.pallas_call(
        _rwkv_prep_kernel,
        out_shape=(jax.ShapeDtypeStruct((s_len, d), F32),) + tuple(jax.ShapeDtypeStruct((s_len, d), BF16) for _ in range(3)),
        grid=(s_len // tm,),
        in_specs=[row, low, low, up, up, vec, vec, vec, vec],
        out_specs=(row, row, row, row),
        compiler_params=_cparams("parallel"),
        name="rwkv_prep",
    )(k, w_lora, a_lora, w2, a2, w0.reshape(1, d), a0.reshape(1, d), k_k.reshape(1, d), k_a.reshape(1, d))


def _rwkv_chunk_kernel(r_ref, lw_ref, k_ref, v_ref, kk_ref, kka_ref, y_ref, s_s, *, c, hb):
    n = RWKV_HEAD

    @pl.when(pl.program_id(1) == 0)
    def _():
        s_s[...] = jnp.zeros(s_s.shape, F32)

    rr = _iota((c, c), 0)
    cc = _iota((c, c), 1)
    lw = lw_ref[...]
    cum = _dot((rr >= cc).astype(F32), lw, HI)
    grow = jnp.exp(-cum)
    decay = jnp.exp(cum)
    total = decay[c - 1:c, :]
    q_t = kk_ref[...].astype(F32) * jnp.exp(cum - lw)
    p_t = -(kka_ref[...].astype(F32) * grow)
    k_t = k_ref[...].astype(F32) * grow
    r_t = r_ref[...].astype(F32) * decay
    p_end = p_t * total
    k_end = k_t * total
    r2 = _iota((2 * c, 2 * c), 0)
    c2 = _iota((2 * c, 2 * c), 1) & (c - 1)
    gram_mask = (((r2 < c) & (r2 > c2)) | ((r2 >= c) & ((r2 - c) >= c2))).astype(F32)
    eye = (rr == cc).astype(F32)
    same_block = [(lax.shift_right_logical(rr, s) == lax.shift_right_logical(cc, s)).astype(F32)
                  for s in range(3, int(math.log2(c)) + 1)]
    heads = range(hb)
    sls = [slice(i * n, (i + 1) * n) for i in heads]
    qr = [jnp.concatenate([q_t[:, sl], r_t[:, sl]], axis=0).astype(BF16) for sl in sls]
    pk = [jnp.concatenate([p_t[:, sl], k_t[:, sl]], axis=0).astype(BF16) for sl in sls]
    state = [s_s[i] for i in heads]
    from_state = [_dot_nt(qr[i], state[i].astype(BF16)) for i in heads]
    gram = [_dot_nt(qr[i], pk[i]) * gram_mask for i in heads]
    l_qp = [g[:c, :c] for g in gram]
    vb = [v_ref[:, sl] for sl in sls]
    from_v = [_dot(jnp.concatenate([gram[i][:c, c:], gram[i][c:, c:]], axis=0).astype(BF16), vb[i]) for i in heads]
    power = [l * same_block[0] for l in l_qp]
    inv = [eye + p for p in power]
    for _ in range(2):
        pb = [p.astype(BF16) for p in power]
        power = [_dot(p, p) for p in pb]
        inv = [x + _dot(x.astype(BF16), p.astype(BF16)) for x, p in zip(inv, power)]
    for lvl in range(1, len(same_block)):
        level_mask = same_block[lvl] - same_block[lvl - 1]
        ib = [x.astype(BF16) for x in inv]
        half = [_dot(ib[i], (l_qp[i] * level_mask).astype(BF16)).astype(BF16) for i in heads]
        inv = [inv[i] + _dot(half[i], ib[i]) for i in heads]
    u = [_dot(inv[i].astype(BF16), (from_state[i][:c] + from_v[i][:c]).astype(BF16)) for i in heads]
    ub = [x.astype(BF16) for x in u]
    for i in heads:
        y_ref[:, sls[i]] = from_state[i][c:] + _dot(gram[i][c:, :c].astype(BF16), ub[i]) + from_v[i][c:]
    for i in heads:
        uv = jnp.concatenate([ub[i], vb[i]], axis=0)
        pk_end = jnp.concatenate([p_end[:, sls[i]], k_end[:, sls[i]]], axis=0).astype(BF16)
        s_s[i] = state[i] * total[:, sls[i]] + _dot_tn(uv, pk_end)


def _rwkv_scan(r, lw, k2, v, kk, kka):
    s_len, d = r.shape
    c, hb = RWKV_CHUNK, RWKV_HB
    width = hb * RWKV_HEAD
    blk = pl.BlockSpec((c, width), lambda hg, ci: (ci, hg))
    return pl.pallas_call(
        functools.partial(_rwkv_chunk_kernel, c=c, hb=hb),
        out_shape=jax.ShapeDtypeStruct((s_len, d), F32),
        grid=(d // width, s_len // c),
        in_specs=[blk] * 6,
        out_specs=blk,
        scratch_shapes=[pltpu.VMEM((hb, RWKV_HEAD, RWKV_HEAD), F32)],
        compiler_params=_cparams("parallel", "arbitrary"),
        name="rwkv_chunk_scan",
    )(r, lw, k2, v, kk, kka)


def _rwkv_post_kernel(y_ref, r_ref, k2_ref, v_ref, z_ref, rk_ref, g_ref, b_ref, o_ref):
    avg = _head_block_diag(1.0 / RWKV_HEAD)
    ones = _head_block_diag(1.0)
    for j in range(y_ref.shape[1] // LANES):
        sl = slice(j * LANES, (j + 1) * LANES)
        y = y_ref[:, sl]
        dev = y - _head_sum(y, avg)
        var = _head_sum(dev * dev, avg)
        yn = dev * lax.rsqrt(var + LNX_EPS) * g_ref[:, sl] + b_ref[:, sl]
        rk = r_ref[:, sl].astype(F32) * k2_ref[:, sl].astype(F32) * rk_ref[:, sl]
        bonus = _head_sum(rk, ones) * v_ref[:, sl].astype(F32)
        o_ref[:, sl] = ((yn + bonus) * _silu(z_ref[:, sl].astype(F32))).astype(o_ref.dtype)


def _rwkv_post(y, r, k2, v, z, r_k, lnx_g, lnx_b):
    s_len, d = y.shape
    tm = min(ROW_TM, s_len)
    row = pl.BlockSpec((tm, d), lambda i: (i, 0))
    vec = pl.BlockSpec((1, d), lambda i: (0, 0))
    return pl.pallas_call(
        _rwkv_post_kernel,
        out_shape=jax.ShapeDtypeStruct((s_len, d), BF16),
        grid=(s_len // tm,),
        in_specs=[row, row, row, row, row, vec, vec, vec],
        out_specs=row,
        compiler_params=_cparams("parallel"),
        name="rwkv_post",
    )(y, r, k2, v, z, r_k.reshape(1, d), lnx_g.reshape(1, d), lnx_b.reshape(1, d))


def _rmsnorm_kernel(x_ref, g_ref, o_ref):
    x = x_ref[...].astype(F32)
    o_ref[...] = (x * lax.rsqrt(jnp.mean(x * x, axis=-1, keepdims=True) + EPS) * g_ref[...]).astype(o_ref.dtype)


def _rmsnorm(x, gain, out_dtype):
    s_len, d = x.shape
    tm = min(ROW_TM, s_len)
    return pl.pallas_call(
        _rmsnorm_kernel,
        out_shape=jax.ShapeDtypeStruct((s_len, d), out_dtype),
        grid=(s_len // tm,),
        in_specs=[pl.BlockSpec((tm, d), lambda i: (i, 0)), pl.BlockSpec((1, d), lambda i: (0, 0))],
        out_specs=pl.BlockSpec((tm, d), lambda i: (i, 0)),
        compiler_params=_cparams("parallel"),
        name="rmsnorm",
    )(x, gain.reshape(1, d))


def _rotate_half_cols(w):
    half = w.shape[1] // 2
    return jnp.concatenate([-w[:, half:], w[:, :half]], axis=1)


def _arrange_w_in(w_in):
    d = w_in.shape[0]
    q, kv, g, z, qa, kva, mlaz = jnp.split(w_in, [1024, 1792, 1840, 2864, 3376, 3952], axis=1)
    ckv, kpe = kva[:, :KV_LORA], kva[:, KV_LORA:]
    pad = jnp.zeros((d, IN0_COLS - C_G - g.shape[1]), w_in.dtype)
    return jnp.concatenate([q, z, mlaz, qa, ckv, kv, kpe, _rotate_half_cols(kpe), g, pad], axis=1).astype(BF16)


def _arrange_w_qb(w_qb):
    w = w_qb.reshape(Q_LORA, MLA_HEADS, QK_NOPE + QK_ROPE)
    nope = w[:, :, :QK_NOPE].reshape(Q_LORA, -1)
    pe = w[:, :, QK_NOPE:]
    half = QK_ROPE // 2
    rot = jnp.concatenate([-pe[:, :, half:], pe[:, :, :half]], axis=2)
    return jnp.concatenate([nope, pe.reshape(Q_LORA, -1), rot.reshape(Q_LORA, -1)], axis=1).astype(BF16)


def _arrange_w_kvb(w_kvb):
    w = w_kvb.reshape(KV_LORA, MLA_HEADS, QK_NOPE + V_HEAD)
    return jnp.concatenate([w[:, :, :QK_NOPE].reshape(KV_LORA, -1), w[:, :, QK_NOPE:].reshape(KV_LORA, -1)],
                           axis=1).astype(BF16)


def _rope_tables(s_len):
    inv = ROPE_THETA ** (-np.arange(0, QK_ROPE, 2, dtype=np.float32) / QK_ROPE)
    ang = np.arange(s_len, dtype=np.float32)[:, None] * inv[None].astype(np.float32)
    cos = np.concatenate([np.cos(ang), np.cos(ang)], axis=1).astype(np.float32)
    sin = np.concatenate([np.sin(ang), np.sin(ang)], axis=1).astype(np.float32)
    return jnp.asarray(cos), jnp.asarray(sin)


def _pad_cols(w, n):
    return jnp.pad(w, ((0, 0), (0, n - w.shape[1])))


def _pad_rows(w, n):
    return jnp.pad(w, ((0, n - w.shape[0]), (0, 0)))


def _attention_layer(x, mod, layer, gain, w_in, w_out, pe_k, w1_k, b1_k, w2_k, pe_v, w1_v, b1_v, w2_v,
                     qa_g, w_qb, kva_g, w_kvb):
    s_len, d = x.shape
    assert s_len % 1024 == 0 and s_len // SLC_LEN <= SEL_LANES
    n_cmp = (s_len - CMP_LEN) // CMP_STRIDE + 1
    n_slc = s_len // SLC_LEN
    proj = _matmul(x, _arrange_w_in(w_in), out_dtype=F32, prologue="rms_mod", gain=gain, mod=mod, layer=layer,
                   tm=IN_PROJ_TM, name="in_proj")
    qfull = _matmul(proj, _arrange_w_qb(w_qb), out_dtype=BF16, a_col=C_QA // Q_LORA, k=Q_LORA, prologue="rms",
                    gain=qa_g, name="mla_q_proj")
    kv = _matmul(proj, _arrange_w_kvb(w_kvb), out_dtype=BF16, a_col=C_CKV // KV_LORA, k=KV_LORA, prologue="rms",
                 gain=kva_g, name="mla_kv_proj")
    cos, sin = _rope_tables(s_len)
    y_mla = _mla_attention(qfull, kv, proj, cos, sin)
    cl = CMP_LEN * NSA_DK
    kcvc = proj[:, C_KV:C_KV + 2 * NSA_GROUPS * NSA_DK].reshape(s_len, 2, NSA_GROUPS, NSA_DK)
    r = kcvc.transpose(1, 2, 0, 3).reshape(2 * NSA_GROUPS, s_len // CMP_STRIDE, CMP_STRIDE * NSA_DK)
    cmp_tokens = _compress_tokens(
        r,
        jnp.stack([w1_k.reshape(cl, -1), w1_v.reshape(cl, -1)]),
        jnp.stack([pe_k.reshape(1, cl), pe_v.reshape(1, cl)]),
        jnp.stack([b1_k.reshape(1, -1), b1_v.reshape(1, -1)]),
        jnp.stack([w2_k, w2_v]),
        n_cmp)
    o_cmp, sel, block_any = _nsa_compressed(proj, cmp_tokens, n_cmp, n_slc)
    o_win = _nsa_flash(proj, "win")
    y_nsa = _nsa_flash(proj, "slc", sel=sel, o_cmp=o_cmp, o_win=o_win, block_any=block_any)
    return _matmul(y_nsa, w_out.astype(BF16), a2=y_mla, out_dtype=F32, epilogue="residual", res=x, mod=mod,
                   layer=layer, name="attn_out_proj")


def _rwkv_layer(x, mod, layer, gain, mu, w_r, w_k, w_v, w_z, w_o, w0, w1, w2, a0, a1, a2, k_k, k_a, r_k,
                lnx_g, lnx_b, final_gain=None):
    xr, xw, xk, xv, xa, xz = _shift_lerp(x, gain, mod, layer, mu)
    r = _matmul(xr, w_r.astype(BF16), out_dtype=BF16, name="rwkv_r")
    k = _matmul(xk, w_k.astype(BF16), out_dtype=BF16, name="rwkv_k")
    v = _matmul(xv, w_v.astype(BF16), out_dtype=BF16, name="rwkv_v")
    z = _matmul(xz, w_z.astype(BF16), out_dtype=BF16, name="rwkv_z")
    w_lora = _matmul(xw, _pad_cols(w1, LORA_PAD).astype(BF16), out_dtype=BF16, epilogue="tanh", name="rwkv_w_lora")
    a_lora = _matmul(xa, _pad_cols(a1, LORA_PAD).astype(BF16), out_dtype=BF16, name="rwkv_a_lora")
    lw, kk, kka, k2 = _rwkv_prep(k, w_lora, a_lora, _pad_rows(w2, LORA_PAD).astype(BF16),
                                 _pad_rows(a2, LORA_PAD).astype(BF16), w0, a0, k_k, k_a)
    y = _rwkv_scan(r, lw, k2, v, kk, kka)
    y = _rwkv_post(y, r, k2, v, z, r_k, lnx_g, lnx_b)
    epilogue = "residual" if final_gain is None else "residual_rms"
    return _matmul(y, w_o.astype(BF16), out_dtype=F32, epilogue=epilogue, res=x, mod=mod, layer=layer,
                   final_gain=final_gain, tm=OUT_RMS_TM if final_gain is not None else None, name="rwkv_out_proj")


def kernel(x, c, norm_g, ada_w, ada_b, final_g, a_w_in, a_w_out, nsa_pe_k, nsa_w1_k, nsa_b1_k, nsa_w2_k, nsa_pe_v, nsa_w1_v, nsa_b1_v, nsa_w2_v, mla_qa_g, mla_w_qb, mla_kva_g, mla_w_kvb, r_mu, r_w_r, r_w_k, r_w_v, r_w_z, r_w_o, r_w0, r_w1, r_w2, r_a0, r_a1, r_a2, r_k_k, r_k_a, r_r_k, r_lnx_g, r_lnx_b):
    batch, s_len, d = x.shape
    depth = ada_w.shape[0]
    outs = []
    for b in range(batch):
        mod = _modulation(c[b:b + 1], ada_w, ada_b)
        xb = x[b]
        for i in range(depth):
            j = i // 2
            if i % 2 == 0:
                xb = _attention_layer(xb, mod, i, norm_g[i], a_w_in[j], a_w_out[j],
                                      nsa_pe_k[j], nsa_w1_k[j], nsa_b1_k[j], nsa_w2_k[j],
                                      nsa_pe_v[j], nsa_w1_v[j], nsa_b1_v[j], nsa_w2_v[j],
                                      mla_qa_g[j], mla_w_qb[j], mla_kva_g[j], mla_w_kvb[j])
            else:
                xb = _rwkv_layer(xb, mod, i, norm_g[i], r_mu[j], r_w_r[j], r_w_k[j], r_w_v[j], r_w_z[j], r_w_o[j],
                                 r_w0[j], r_w1[j], r_w2[j], r_a0[j], r_a1[j], r_a2[j], r_k_k[j], r_k_a[j],
                                 r_r_k[j], r_lnx_g[j], r_lnx_b[j], final_gain=final_g if i == depth - 1 else None)
        outs.append(xb if depth % 2 == 0 else _rmsnorm(xb, final_g, x.dtype))
    return jnp.stack(outs)
```

```python
import functools
import math

import jax
import jax.numpy as jnp
import numpy as np
from jax import lax
from jax.experimental import pallas as pl
from jax.experimental.pallas import tpu as pltpu

F32 = jnp.float32
BF16 = jnp.bfloat16
HI = lax.Precision.HIGHEST

EPS = 1e-6
NEG_INF = -1e30
FORCE = 1e9
NSA_HEADS = 16
NSA_GROUPS = 2
NSA_HPG = NSA_HEADS // NSA_GROUPS
NSA_DK = 64
CMP_LEN = 32
CMP_STRIDE = 16
SLC_LEN = 64
SLC_TOPK = 16
WINDOW = 512
MLA_HEADS = 8
Q_LORA = 512
KV_LORA = 512
QK_NOPE = 128
QK_ROPE = 64
V_HEAD = 128
ROPE_THETA = 10000.0
RWKV_HEAD = 64
LORA_PAD = 128
LNX_EPS = 64e-5

LANES = 128
VMEM_LIMIT_BYTES = 48 * 1024 * 1024

MM_TM = 1024
MM_TM_F32 = 512
MM_TN = 1024
IN_PROJ_TM = 1024
OUT_RMS_TM = 512
ROW_TM = 256
MLA_T = 512
NSA_TQ = 128
NSA_TK = 512
NSA_SLOTS = 4
RWKV_CHUNK = 64
RWKV_HB = 32
SEL_LANES = 128
SM_ROWS = 64

LOG2E = 1.0 / math.log(2.0)
M_FLOOR = -1e20

C_Q = 0
C_Z = 1024
C_MLAZ = 2048
C_QA = 3072
C_CKV = 3584
C_KV = 4096
C_KPE = 4864
C_G = 4992
IN0_COLS = 5120


def _cparams(*sem):
    return pltpu.CompilerParams(dimension_semantics=sem, vmem_limit_bytes=VMEM_LIMIT_BYTES)


def _dot(a, b, precision=None):
    return jnp.dot(a, b, preferred_element_type=F32, precision=precision)


def _dot_nt(a, b):
    return lax.dot_general(a, b, (((1,), (1,)), ((), ())), preferred_element_type=F32)


def _dot_tn(a, b):
    return lax.dot_general(a, b, (((0,), (0,)), ((), ())), preferred_element_type=F32)


def _silu(x):
    return x * jax.nn.sigmoid(x)


def _iota(shape, dim):
    return lax.broadcasted_iota(jnp.int32, shape, dim)


def _lane_tile_reduce(x, op):
    parts = [x[:, i * LANES:(i + 1) * LANES] for i in range(x.shape[1] // LANES)]
    while len(parts) > 1:
        parts = [op(parts[i], parts[i + 1]) if i + 1 < len(parts) else parts[i] for i in range(0, len(parts), 2)]
    return parts[0]


def _mod_kernel(c_ref, w_ref, b_ref, o_ref):
    c = c_ref[...]
    sc = jnp.broadcast_to(_silu(c), (8, c.shape[1]))
    o_ref[0] = _dot(sc, w_ref[0], HI) + b_ref[0]


def _modulation(c, ada_w, ada_b):
    depth, d, n = ada_w.shape
    tn = 768
    return pl.pallas_call(
        _mod_kernel,
        out_shape=jax.ShapeDtypeStruct((depth, 8, n), F32),
        grid=(depth, n // tn),
        in_specs=[
            pl.BlockSpec((1, d), lambda i, j: (0, 0)),
            pl.BlockSpec((1, d, tn), lambda i, j: (i, 0, j)),
            pl.BlockSpec((1, 1, tn), lambda i, j: (i, 0, j)),
        ],
        out_specs=pl.BlockSpec((1, 8, tn), lambda i, j: (i, 0, j)),
        compiler_params=_cparams("parallel", "parallel"),
        name="adaln_mod",
    )(c, ada_w, ada_b.reshape(depth, 1, n))


def _mm_kernel(*refs, prologue, epilogue, split_k):
    it = iter(refs)
    a_ref = next(it)
    a2_ref = next(it) if split_k else None
    b_ref = next(it)
    g_ref = next(it) if prologue != "none" else None
    sc_ref, sh_ref = (next(it), next(it)) if prologue == "rms_mod" else (None, None)
    res_ref, gate_ref = (next(it), next(it)) if epilogue.startswith("residual") else (None, None)
    fg_ref = next(it) if epilogue == "residual_rms" else None
    o_ref = next(it)
    if prologue == "none":
        a = a_ref[...]
    else:
        a_s = next(it)

        @pl.when(pl.program_id(1) == 0)
        def _():
            x = a_ref[...].astype(F32)
            y = x * lax.rsqrt(jnp.mean(x * x, axis=-1, keepdims=True) + EPS) * g_ref[...]
            if prologue == "rms_mod":
                y = y * (1.0 + sc_ref[0, 0:1, :]) + sh_ref[0, 0:1, :]
            a_s[...] = y.astype(BF16)

        a = a_s[...]
    if split_k:
        k1 = a.shape[1]
        acc = _dot(a, b_ref[0:k1, :]) + _dot(a2_ref[...], b_ref[k1:, :])
    else:
        acc = _dot(a, b_ref[...])
    if epilogue.startswith("residual"):
        acc = res_ref[...] + gate_ref[0, 0:1, :] * acc
        if epilogue == "residual_rms":
            acc = acc * lax.rsqrt(jnp.mean(acc * acc, axis=-1, keepdims=True) + EPS) * fg_ref[...]
    o_ref[...] = acc.astype(o_ref.dtype)


def _matmul(a, b, *, out_dtype, a2=None, a_col=0, k=None, prologue="none", gain=None, mod=None, layer=0,
            epilogue="none", res=None, final_gain=None, tm=None, tn=None, name="matmul"):
    m = a.shape[0]
    n = b.shape[1]
    k = (b.shape[0] if a2 is None else a.shape[1]) if k is None else k
    d_mod = None if mod is None else mod.shape[2] // 3
    tm = min(tm or (MM_TM if prologue == "none" else MM_TM_F32), m)
    tn = n if epilogue == "residual_rms" else min(tn or MM_TN, n)
    in_specs = [pl.BlockSpec((tm, k), lambda i, j: (i, a_col))]
    args = [a]
    if a2 is not None:
        in_specs.append(pl.BlockSpec((tm, a2.shape[1]), lambda i, j: (i, 0)))
        args.append(a2)
    in_specs.append(pl.BlockSpec((b.shape[0], tn), lambda i, j: (0, j)))
    args.append(b)
    scratch = []
    if prologue != "none":
        in_specs.append(pl.BlockSpec((1, k), lambda i, j: (0, 0)))
        args.append(gain.reshape(1, k).astype(F32))
        scratch.append(pltpu.VMEM((tm, k), BF16))
    if prologue == "rms_mod":
        in_specs += [pl.BlockSpec((1, 8, k), lambda i, j: (layer, 0, 1)), pl.BlockSpec((1, 8, k), lambda i, j: (layer, 0, 0))]
        args += [mod, mod]
    if epilogue.startswith("residual"):
        in_specs += [pl.BlockSpec((tm, tn), lambda i, j: (i, j)),
                     pl.BlockSpec((1, 8, tn), lambda i, j: (layer, 0, 2 * (d_mod // tn) + j))]
        args += [res, mod]
    if epilogue == "residual_rms":
        in_specs.append(pl.BlockSpec((1, n), lambda i, j: (0, 0)))
        args.append(final_gain.reshape(1, n))
    return pl.pallas_call(
        functools.partial(_mm_kernel, prologue=prologue, epilogue=epilogue, split_k=a2 is not None),
        out_shape=jax.ShapeDtypeStruct((m, n), out_dtype),
        grid=(m // tm, n // tn),
        in_specs=in_specs,
        out_specs=pl.BlockSpec((tm, tn), lambda i, j: (i, j)),
        scratch_shapes=scratch,
        compiler_params=_cparams("parallel", "arbitrary"),
        name=name,
    )(*args)


def _mla_attn_kernel(qt_ref, kt_ref, qn_ref, qp_ref, qpr_ref, cq_ref, sq_ref, k_ref, v_ref, kpe_ref, ck_ref, sk_ref, z_ref,
                     o_ref, q_s, kc_s, s_s, p_s, a_s, m_s, l_s, acc_s, *, t, scale):
    qi = qt_ref[pl.program_id(0)]
    ki = kt_ref[pl.program_id(0)]
    dq = QK_NOPE + QK_ROPE

    @pl.when(ki == 0)
    def _():
        qn = (qn_ref[...].astype(F32) * (scale * LOG2E)).astype(BF16)
        cos = cq_ref[...] * (scale * LOG2E)
        sin = sq_ref[...] * (scale * LOG2E)
        for h in range(MLA_HEADS):
            pe = slice(h * QK_ROPE, (h + 1) * QK_ROPE)
            qp = qp_ref[:, pe].astype(F32) * cos + qpr_ref[:, pe].astype(F32) * sin
            q_s[:, h * dq:h * dq + QK_NOPE] = qn[:, h * QK_NOPE:(h + 1) * QK_NOPE]
            q_s[:, h * dq + QK_NOPE:(h + 1) * dq] = qp.astype(BF16)
        m_s[...] = jnp.full(m_s.shape, M_FLOOR, F32)
        l_s[...] = jnp.zeros(l_s.shape, F32)
        acc_s[...] = jnp.zeros(acc_s.shape, F32)

    def step(diagonal):
        kpe = kpe_ref[...]
        kp = (kpe[:, :QK_ROPE] * ck_ref[...] + kpe[:, QK_ROPE:] * sk_ref[...]).astype(BF16)
        for h in range(MLA_HEADS):
            kc_s[h, :, 0:QK_NOPE] = k_ref[:, h * QK_NOPE:(h + 1) * QK_NOPE]
            kc_s[h, :, QK_NOPE:dq] = kp
        for h in range(MLA_HEADS):
            s = _dot_nt(q_s[:, h * dq:(h + 1) * dq], kc_s[h])
            if diagonal:
                s = jnp.where(_iota((t, 1), 0) >= _iota((1, t), 1), s, NEG_INF)
            s_s[h] = s
            row_max = jnp.max(_lane_tile_reduce(s, jnp.maximum), axis=1, keepdims=True)
            m_prev = m_s[h]
            m_new = jnp.maximum(m_prev, jnp.broadcast_to(row_max, (t, LANES)))
            a_s[h] = jnp.exp2(m_prev - m_new)
            m_s[h] = m_new
        for h in range(MLA_HEADS):
            for rb in range(t // SM_ROWS):
                r = slice(rb * SM_ROWS, (rb + 1) * SM_ROWS)
                p = jnp.exp2(s_s[h, r] - jnp.tile(m_s[h, r], (1, t // LANES)))
                p_s[h, r] = p.astype(BF16)
                row_sum = jnp.sum(_lane_tile_reduce(p, jnp.add), axis=1, keepdims=True)
                l_s[h, r] = a_s[h, r] * l_s[h, r] + jnp.broadcast_to(row_sum, (SM_ROWS, LANES))
        for h in range(MLA_HEADS):
            acc_s[h] = a_s[h] * acc_s[h] + _dot(p_s[h], v_ref[:, h * V_HEAD:(h + 1) * V_HEAD])

    @pl.when(ki < qi)
    def _():
        step(False)

    @pl.when(ki == qi)
    def _():
        step(True)
        for h in range(MLA_HEADS):
            sl = slice(h * V_HEAD, (h + 1) * V_HEAD)
            o_ref[:, sl] = (acc_s[h] / l_s[h] * _silu(z_ref[:, sl])).astype(o_ref.dtype)


def _mla_attention(qfull, kv, proj, cos, sin):
    s_len = qfull.shape[0]
    t = min(MLA_T, s_len)
    n = s_len // t
    hn = MLA_HEADS * QK_NOPE
    hp = MLA_HEADS * QK_ROPE
    hv = MLA_HEADS * V_HEAD
    pairs = [(qi, ki) for qi in range(n) for ki in range(qi + 1)]
    q_tiles = jnp.asarray([p[0] for p in pairs], jnp.int32)
    k_tiles = jnp.asarray([p[1] for p in pairs], jnp.int32)
    q_blk = lambda col: (lambda i, qt, kt: (qt[i], col))
    k_blk = lambda col: (lambda i, qt, kt: (kt[i], col))
    grid_spec = pltpu.PrefetchScalarGridSpec(
        num_scalar_prefetch=2,
        grid=(len(pairs),),
        in_specs=[
            pl.BlockSpec((t, hn), q_blk(0)),
            pl.BlockSpec((t, hp), q_blk(hn // hp)),
            pl.BlockSpec((t, hp), q_blk(hn // hp + 1)),
            pl.BlockSpec((t, QK_ROPE), q_blk(0)),
            pl.BlockSpec((t, QK_ROPE), q_blk(0)),
            pl.BlockSpec((t, hn), k_blk(0)),
            pl.BlockSpec((t, hv), k_blk(1)),
            pl.BlockSpec((t, 2 * QK_ROPE), k_blk(C_KPE // (2 * QK_ROPE))),
            pl.BlockSpec((t, QK_ROPE), k_blk(0)),
            pl.BlockSpec((t, QK_ROPE), k_blk(0)),
            pl.BlockSpec((t, hv), q_blk(C_MLAZ // hv)),
        ],
        out_specs=pl.BlockSpec((t, hv), q_blk(0)),
        scratch_shapes=[
            pltpu.VMEM((t, hn + hp), BF16),
            pltpu.VMEM((MLA_HEADS, t, QK_NOPE + QK_ROPE), BF16),
            pltpu.VMEM((MLA_HEADS, t, t), F32),
            pltpu.VMEM((MLA_HEADS, t, t), BF16),
            pltpu.VMEM((MLA_HEADS, t, LANES), F32),
            pltpu.VMEM((MLA_HEADS, t, LANES), F32),
            pltpu.VMEM((MLA_HEADS, t, LANES), F32),
            pltpu.VMEM((MLA_HEADS, t, V_HEAD), F32),
        ],
    )
    return pl.pallas_call(
        functools.partial(_mla_attn_kernel, t=t, scale=(QK_NOPE + QK_ROPE) ** -0.5),
        out_shape=jax.ShapeDtypeStruct((s_len, hv), BF16),
        grid_spec=grid_spec,
        compiler_params=_cparams("arbitrary"),
        name="mla_attention",
    )(q_tiles, k_tiles, qfull, qfull, qfull, cos, sin, kv, kv, proj, cos, sin, proj)


def _cmp_tokens_kernel(r_ref, w1_ref, pe_ref, b1_ref, w2_ref, o_ref, *, n_cmp):
    r = r_ref[0]
    w1 = w1_ref[0]
    half = w1.shape[0] // 2
    first = _dot(r, w1[:half], HI)
    second = _dot(r, w1[half:], HI)
    rows = r.shape[0]
    second = pltpu.roll(second, rows - 1, 0)
    hid = first + second + _dot(pe_ref[0], w1, HI) + b1_ref[0]
    out = _dot(_silu(hid), w2_ref[0], HI)
    o_ref[0] = jnp.where(_iota((rows, 1), 0) < n_cmp, out, 0.0)


def _compress_tokens(r, w1, pe, b1, w2, n_cmp):
    four, rows, width = r.shape
    hid = w1.shape[2]
    dk = w2.shape[2]
    return pl.pallas_call(
        functools.partial(_cmp_tokens_kernel, n_cmp=n_cmp),
        out_shape=jax.ShapeDtypeStruct((four, rows, dk), F32),
        grid=(four,),
        in_specs=[
            pl.BlockSpec((1, rows, width), lambda i: (i, 0, 0)),
            pl.BlockSpec((1, 2 * width, hid), lambda i: (i // NSA_GROUPS, 0, 0)),
            pl.BlockSpec((1, 1, 2 * width), lambda i: (i // NSA_GROUPS, 0, 0)),
            pl.BlockSpec((1, 1, hid), lambda i: (i // NSA_GROUPS, 0, 0)),
            pl.BlockSpec((1, hid, dk), lambda i: (i // NSA_GROUPS, 0, 0)),
        ],
        out_specs=pl.BlockSpec((1, rows, dk), lambda i: (i, 0, 0)),
        compiler_params=_cparams("parallel"),
        name="nsa_compress_tokens",
    )(r, w1, pe, b1, w2)


def _alibi_slope(h):
    return 2.0 ** (-8.0 * (h + 1) / NSA_HEADS)


def _bf16_parts(x, n=3):
    parts = []
    for _ in range(n):
        hi = float(np.asarray(x, dtype=BF16).astype(np.float32))
        parts.append(hi)
        x = x - hi
    return parts


def _nsa_cmp_kernel(q_ref, g_ref, cmp_ref, o_ref, sel_ref, any_ref, qa_s, ka_s, va_s, s_s, m_s, ehi_s, elo_s,
                    *, tq, n_cmp, n_slc):
    qi = pl.program_id(0)
    n_pad = cmp_ref.shape[1]
    t_col = qi * tq + _iota((tq, 1), 0)
    n_row = _iota((1, n_pad), 1)
    cmp_end = n_row * CMP_STRIDE + (CMP_LEN - 1)
    bias = jnp.where((cmp_end <= t_col) & (n_row < n_cmp), 0.0, NEG_INF)
    gates = jax.nn.sigmoid(g_ref[...])
    lane = _iota((n_pad, NSA_DK), 1)
    pos = _iota((n_pad, NSA_DK), 0) * CMP_STRIDE + (CMP_LEN - 1)
    upper = (pos & ~255).astype(F32)
    lower = (pos & 255).astype(F32)
    feat = jnp.where(lane < 3, upper, jnp.where(lane < 6, lower, 0.0)).astype(BF16)
    ones_col = jnp.where(lane == 0, 1.0, 0.0).astype(BF16)
    for g in range(NSA_GROUPS):
        ka_s[g, :, 0:NSA_DK] = cmp_ref[g].astype(BF16)
        ka_s[g, :, NSA_DK:2 * NSA_DK] = feat
        va_s[g, :, 0:NSA_DK] = cmp_ref[NSA_GROUPS + g].astype(BF16)
        va_s[g, :, NSA_DK:2 * NSA_DK] = ones_col
    qlane = _iota((1, NSA_DK), 1)
    for h in range(NSA_HEADS):
        qa_s[h, :, 0:NSA_DK] = (q_ref[:, h * NSA_DK:(h + 1) * NSA_DK] * (NSA_DK ** -0.5 * LOG2E)).astype(BF16)
        qfeat = jnp.zeros((1, NSA_DK), F32)
        for i, part in enumerate(2 * _bf16_parts(LOG2E * _alibi_slope(h))):
            qfeat = jnp.where(qlane == i, part, qfeat)
        qa_s[h, :, NSA_DK:2 * NSA_DK] = jnp.broadcast_to(qfeat, (tq, NSA_DK)).astype(BF16)
    n_c = _iota((n_pad, SEL_LANES), 0) * CMP_STRIDE
    j_c = _iota((n_pad, SEL_LANES), 1) * SLC_LEN
    overlap = ((n_c < j_c + SLC_LEN) & (n_c + CMP_LEN > j_c)).astype(BF16)
    j_row = _iota((1, SEL_LANES), 1)
    cur = lax.shift_right_logical(t_col, int(math.log2(SLC_LEN)))
    forced = (j_row == 0) | (j_row == cur) | (j_row == cur - 1)
    causal = (j_row * SLC_LEN <= t_col) & (j_row < n_slc)
    j_t = _iota((SEL_LANES, tq), 0)
    for h in range(NSA_HEADS):
        s = _dot_nt(qa_s[h], ka_s[h // NSA_HPG]) + bias
        s_s[h] = s
        row_max = jnp.max(_lane_tile_reduce(s, jnp.maximum), axis=1, keepdims=True)
        m_s[h] = jnp.maximum(jnp.broadcast_to(row_max, (tq, LANES)), M_FLOOR)
    for h in range(NSA_HEADS):
        e = jnp.exp2(s_s[h] - jnp.tile(m_s[h], (1, n_pad // LANES)))
        e_hi = e.astype(BF16)
        ehi_s[h] = e_hi
        elo_s[h] = (e - e_hi.astype(F32)).astype(BF16)
    for g in range(NSA_GROUPS):
        imp = jnp.zeros((tq, SEL_LANES), F32)
        for hh in range(NSA_HPG):
            h = g * NSA_HPG + hh
            sl = slice(h * NSA_DK, (h + 1) * NSA_DK)
            pv = _dot(ehi_s[h], va_s[g]) + _dot(elo_s[h], va_s[g])
            den = pv[:, NSA_DK:NSA_DK + 1]
            inv_den = 1.0 / jnp.where(den > 0.0, den, 1.0)
            o_ref[:, sl] = pv[:, 0:NSA_DK] * (inv_den * gates[:, 3 * h:3 * h + 1])
            imp = imp + (_dot(ehi_s[h], overlap) + _dot(elo_s[h], overlap)) * inv_den
        imp = jnp.where(forced, FORCE, imp)
        imp = jnp.where(causal, imp, jnp.where(j_row < n_slc, NEG_INF, -3e38))
        imp_t = imp.T
        sel_t = jnp.zeros((SEL_LANES, tq), F32)
        for _ in range(min(SLC_TOPK, n_slc)):
            best = jnp.max(imp_t, axis=0, keepdims=True)
            first = jnp.min(jnp.where(imp_t == best, j_t, SEL_LANES), axis=0, keepdims=True)
            hit = j_t == first
            sel_t = jnp.where(hit, 1.0, sel_t)
            imp_t = jnp.where(hit, -3.4e38, imp_t)
        sel = sel_t.T
        sel_ref[:, g * SEL_LANES:(g + 1) * SEL_LANES] = sel.astype(sel_ref.dtype)
        tile_any = jnp.max(sel, axis=0, keepdims=True)
        picked_any = tile_any if g == 0 else jnp.maximum(picked_any, tile_any)
    any_ref[0] = jnp.broadcast_to(picked_any, (8, SEL_LANES))


def _nsa_compressed(proj, cmp_tokens, n_cmp, n_slc):
    s_len = proj.shape[0]
    tq = NSA_TQ
    hd = NSA_HEADS * NSA_DK
    return pl.pallas_call(
        functools.partial(_nsa_cmp_kernel, tq=tq, n_cmp=n_cmp, n_slc=n_slc),
        out_shape=(jax.ShapeDtypeStruct((s_len, hd), F32),
                   jax.ShapeDtypeStruct((s_len, NSA_GROUPS * SEL_LANES), BF16),
                   jax.ShapeDtypeStruct((s_len // tq, 8, SEL_LANES), F32)),
        grid=(s_len // tq,),
        in_specs=[
            pl.BlockSpec((tq, hd), lambda i: (i, C_Q // hd)),
            pl.BlockSpec((tq, LANES), lambda i: (i, C_G // LANES)),
            pl.BlockSpec(cmp_tokens.shape, lambda i: (0, 0, 0)),
        ],
        out_specs=(pl.BlockSpec((tq, hd), lambda i: (i, 0)),
                   pl.BlockSpec((tq, NSA_GROUPS * SEL_LANES), lambda i: (i, 0)),
                   pl.BlockSpec((1, 8, SEL_LANES), lambda i: (i, 0, 0))),
        scratch_shapes=[
            pltpu.VMEM((NSA_HEADS, tq, 2 * NSA_DK), BF16),
            pltpu.VMEM((NSA_GROUPS, cmp_tokens.shape[1], 2 * NSA_DK), BF16),
            pltpu.VMEM((NSA_GROUPS, cmp_tokens.shape[1], 2 * NSA_DK), BF16),
            pltpu.VMEM((NSA_HEADS, tq, cmp_tokens.shape[1]), F32),
            pltpu.VMEM((NSA_HEADS, tq, LANES), F32),
            pltpu.VMEM((NSA_HEADS, tq, cmp_tokens.shape[1]), BF16),
            pltpu.VMEM((NSA_HEADS, tq, cmp_tokens.shape[1]), BF16),
        ],
        compiler_params=_cparams("parallel"),
        name="nsa_compressed_topk",
    )(proj, proj, cmp_tokens)


def _nsa_tile_range(mode, qi, tq, tk):
    q0 = qi * tq
    last = lax.div(q0 + tq - 1, tk)
    if mode == "slc":
        return jnp.zeros_like(last), last
    return lax.div(jnp.maximum(q0 - (WINDOW - 1), 0), tk), last


def _nsa_flash_kernel(*refs, mode, tq, tk, branch, slots):
    if mode == "slc":
        tiles_ref, count_ref = refs[:2]
        q_ref, g_ref = refs[2:4]
        k_refs = refs[4:4 + slots]
        v_refs = refs[4 + slots:4 + 2 * slots]
        sel_ref, ocmp_ref, owin_ref, z_ref, o_ref, q_s, s_s, p_s, a_s, m_s, acc_s = refs[4 + 2 * slots:]
    else:
        q_ref, g_ref, k_ref, v_ref, o_ref, q_s, s_s, p_s, a_s, m_s, acc_s = refs
    qi = pl.program_id(0)
    step = pl.program_id(1)
    rows = NSA_HPG * tq
    feat_lane = [NSA_DK * (1 - g) for g in range(NSA_GROUPS)]

    @pl.when(step == 0)
    def _():
        lane = _iota((1, NSA_DK), 1)
        for h in range(NSA_HEADS):
            g, hh = divmod(h, NSA_HPG)
            r = slice(hh * tq, (hh + 1) * tq)
            q = q_ref[:, h * NSA_DK:(h + 1) * NSA_DK] * (NSA_DK ** -0.5 * LOG2E)
            q_s[g, r, g * NSA_DK:(g + 1) * NSA_DK] = q.astype(BF16)
            feat = jnp.zeros((1, NSA_DK), F32)
            for i, part in enumerate(3 * _bf16_parts(LOG2E * _alibi_slope(h))):
                feat = jnp.where(lane == i, part, feat)
            q_s[g, r, feat_lane[g]:feat_lane[g] + NSA_DK] = jnp.broadcast_to(feat, (tq, NSA_DK)).astype(BF16)
        m_s[...] = jnp.full(m_s.shape, M_FLOOR, F32)
        acc_s[...] = jnp.zeros(acc_s.shape, F32)

    def process(k_ref, v_ref, kt):
        k0 = kt * tk
        delta = (qi * tq + _iota((tq, 1), 0)) - (k0 + _iota((1, tk), 1))
        if mode == "slc":
            blk = lax.div(k0, SLC_LEN) + lax.shift_right_logical(_iota((SEL_LANES, tk), 1), int(math.log2(SLC_LEN)))
            expand = (_iota((SEL_LANES, tk), 0) == blk).astype(BF16)
        else:
            bias = jnp.where((delta >= 0) & (delta < WINDOW), 0.0, NEG_INF)
        jj = _iota((tk, LANES), 0)
        lane = _iota((tk, LANES), 1)
        tile_off = (k0 - qi * tq).astype(F32)
        upper = jnp.where(jj >= 256, 256.0, 0.0)
        lower = (jj & 255).astype(F32)
        kblk = k_ref[...]
        vblk = v_ref[...]
        for g in range(NSA_GROUPS):
            if mode == "slc":
                picked = _dot(sel_ref[:, g * SEL_LANES:(g + 1) * SEL_LANES], expand)
                bias = jnp.where((picked > 0.5) & (delta >= 0), 0.0, NEG_INF)
            f = lane - feat_lane[g]
            data = lax.shift_right_logical(lane, int(math.log2(NSA_DK))) == g
            k_aug = jnp.where(data, kblk, 0.0)
            k_aug = jnp.where((f >= 0) & (f < 3), tile_off, k_aug)
            k_aug = jnp.where((f >= 3) & (f < 6), upper, k_aug)
            k_aug = jnp.where((f >= 6) & (f < 9), lower, k_aug)
            s = _dot_nt(q_s[g], k_aug.astype(BF16))
            for hh in range(NSA_HPG):
                r = slice(hh * tq, (hh + 1) * tq)
                sb = s[r] + bias
                s_s[g, r] = sb
                row_max = jnp.max(_lane_tile_reduce(sb, jnp.maximum), axis=1, keepdims=True)
                m_prev = m_s[g, r]
                m_new = jnp.maximum(m_prev, jnp.broadcast_to(row_max, (tq, LANES)))
                a_s[g, r] = jnp.exp2(m_prev - m_new)
                m_s[g, r] = m_new
        for g in range(NSA_GROUPS):
            for rb in range(rows // SM_ROWS):
                r = slice(rb * SM_ROWS, (rb + 1) * SM_ROWS)
                p_s[g, r] = jnp.exp2(s_s[g, r] - jnp.tile(m_s[g, r], (1, tk // LANES))).astype(BF16)
        for g in range(NSA_GROUPS):
            data = lax.shift_right_logical(lane, int(math.log2(NSA_DK))) == g
            v_aug = jnp.where(lane == feat_lane[g], 1.0, jnp.where(data, vblk, 0.0)).astype(BF16)
            acc_s[g] = a_s[g] * acc_s[g] + _dot(p_s[g], v_aug)

    if mode == "slc":
        per_tile = pl.num_programs(1) * slots
        for slot in range(slots):
            entry = step * slots + slot

            @pl.when(entry < count_ref[qi])
            def _(slot=slot, entry=entry):
                process(k_refs[slot], v_refs[slot], tiles_ref[qi * per_tile + entry])
    else:
        lo, hi = _nsa_tile_range(mode, qi, tq, tk)

        @pl.when(lo + step <= hi)
        def _():
            process(k_ref, v_ref, lo + step)

    @pl.when(step == pl.num_programs(1) - 1)
    def _():
        hd = NSA_HEADS * NSA_DK
        pick_gate = (_iota((LANES, hd), 0)
                     == 3 * lax.shift_right_logical(_iota((LANES, hd), 1), int(math.log2(NSA_DK))) + branch).astype(BF16)
        gate_all = _head_sum(jax.nn.sigmoid(g_ref[...]), pick_gate)
        for h in range(NSA_HEADS):
            g, hh = divmod(h, NSA_HPG)
            r = slice(hh * tq, (hh + 1) * tq)
            sl = slice(h * NSA_DK, (h + 1) * NSA_DK)
            acc = acc_s[g, r]
            pick_total = (_iota((LANES, NSA_DK), 0) == feat_lane[g]).astype(BF16)
            o = acc[:, g * NSA_DK:(g + 1) * NSA_DK] / _head_sum(acc, pick_total) * gate_all[:, sl]
            if mode == "slc":
                o = (o + ocmp_ref[:, sl] + owin_ref[:, sl]) * _silu(z_ref[:, sl])
            o_ref[:, sl] = o.astype(o_ref.dtype)


def _nsa_flash(proj, mode, sel=None, o_cmp=None, o_win=None, block_any=None):
    s_len = proj.shape[0]
    tq, tk = NSA_TQ, min(NSA_TK, s_len)
    assert tk % LANES == 0 and tk <= 512
    hd = NSA_HEADS * NSA_DK
    gd = NSA_GROUPS * NSA_DK
    k_slab, v_slab, branch = (2, 3, 1) if mode == "slc" else (4, 5, 2)
    nq = s_len // tq
    nsteps = s_len // tk if mode == "slc" else -(-(WINDOW - 1) // tk) + 1
    rows = NSA_HPG * tq
    scratch = [
        pltpu.VMEM((NSA_GROUPS, rows, LANES), BF16),
        pltpu.VMEM((NSA_GROUPS, rows, tk), F32),
        pltpu.VMEM((NSA_GROUPS, rows, tk), BF16),
        pltpu.VMEM((NSA_GROUPS, rows, LANES), F32),
        pltpu.VMEM((NSA_GROUPS, rows, LANES), F32),
        pltpu.VMEM((NSA_GROUPS, rows, LANES), F32),
    ]
    out_shape = jax.ShapeDtypeStruct((s_len, hd), BF16 if mode == "slc" else F32)
    if mode == "win":
        def kv_map(slab):
            def index(qi, step):
                lo, hi = _nsa_tile_range(mode, qi, tq, tk)
                return jnp.minimum(lo + step, hi), C_KV // gd + slab
            return index

        q_map = lambda qi, step: (qi, 0)
        return pl.pallas_call(
            functools.partial(_nsa_flash_kernel, mode=mode, tq=tq, tk=tk, branch=branch, slots=1),
            out_shape=out_shape, grid=(nq, nsteps),
            in_specs=[
                pl.BlockSpec((tq, hd), lambda qi, step: (qi, C_Q // hd)),
                pl.BlockSpec((tq, LANES), lambda qi, step: (qi, C_G // LANES)),
                pl.BlockSpec((tk, gd), kv_map(k_slab)),
                pl.BlockSpec((tk, gd), kv_map(v_slab)),
            ],
            out_specs=pl.BlockSpec((tq, hd), q_map),
            scratch_shapes=scratch,
            compiler_params=_cparams("parallel", "arbitrary"),
            name="nsa_window",
        )(proj, proj, proj, proj)
    ntiles = s_len // tk
    slots = min(NSA_SLOTS, ntiles)
    blocks_per_tile = tk // SLC_LEN
    used = block_any[:, 0, :ntiles * blocks_per_tile].reshape(nq, ntiles, blocks_per_tile).max(axis=-1) > 0.5
    tile_ids = jnp.arange(ntiles, dtype=jnp.int32)
    causal = tile_ids[None, :] * tk <= (jnp.arange(nq, dtype=jnp.int32)[:, None] * tq + tq - 1)
    visit = used & causal
    count = jnp.maximum(visit.sum(axis=1).astype(jnp.int32), 1)
    order = jnp.argsort(jnp.where(visit, tile_ids[None, :], ntiles + tile_ids[None, :]), axis=1).astype(jnp.int32)
    tiles = jnp.take_along_axis(order, jnp.minimum(tile_ids[None, :], count[:, None] - 1), axis=1).reshape(-1)

    def kv_map(slab, slot):
        def index(qi, step, tiles_ref, count_ref):
            return tiles_ref[qi * ntiles + step * slots + slot], C_KV // gd + slab
        return index

    q_map = lambda qi, step, t, c: (qi, 0)
    grid_spec = pltpu.PrefetchScalarGridSpec(
        num_scalar_prefetch=2,
        grid=(nq, ntiles // slots),
        in_specs=[
            pl.BlockSpec((tq, hd), lambda qi, step, t, c: (qi, C_Q // hd)),
            pl.BlockSpec((tq, LANES), lambda qi, step, t, c: (qi, C_G // LANES)),
            *[pl.BlockSpec((tk, gd), kv_map(k_slab, slot)) for slot in range(slots)],
            *[pl.BlockSpec((tk, gd), kv_map(v_slab, slot)) for slot in range(slots)],
            pl.BlockSpec((tq, NSA_GROUPS * SEL_LANES), q_map),
            pl.BlockSpec((tq, hd), q_map),
            pl.BlockSpec((tq, hd), q_map),
            pl.BlockSpec((tq, hd), lambda qi, step, t, c: (qi, C_Z // hd)),
        ],
        out_specs=pl.BlockSpec((tq, hd), q_map),
        scratch_shapes=scratch,
    )
    return pl.pallas_call(
        functools.partial(_nsa_flash_kernel, mode=mode, tq=tq, tk=tk, branch=branch, slots=slots),
        out_shape=out_shape, grid_spec=grid_spec,
        compiler_params=_cparams("parallel", "arbitrary"),
        name="nsa_selected",
    )(tiles, count, proj, proj, *([proj] * (2 * slots)), sel, o_cmp, o_win, proj)


def _shift_lerp_kernel(x_ref, xp_ref, g_ref, sc_ref, sh_ref, mu_ref, w1_ref, a1_ref, *o_refs, tm):
    def modulated(x):
        y = x * lax.rsqrt(jnp.mean(x * x, axis=-1, keepdims=True) + EPS) * g_ref[...]
        return y * (1.0 + sc_ref[0, 0:1, :]) + sh_ref[0, 0:1, :]

    h = modulated(x_ref[...])
    before = modulated(xp_ref[...])[7:8, :] * jnp.where(pl.program_id(0) > 0, 1.0, 0.0)
    prev = jnp.where(_iota((tm, 1), 0) == 0, before, pltpu.roll(h, 1, 0))
    xx = prev - h
    xr_o, xk_o, xv_o, xz_o, wl_o, al_o = o_refs
    for j, o_ref in ((0, xr_o), (2, xk_o), (3, xv_o), (5, xz_o)):
        o_ref[...] = (h + xx * mu_ref[j:j + 1, :]).astype(o_ref.dtype)
    xw = (h + xx * mu_ref[1:2, :]).astype(BF16)
    xa = (h + xx * mu_ref[4:5, :]).astype(BF16)
    wl_o[...] = jnp.tanh(_dot(xw, w1_ref[...])).astype(wl_o.dtype)
    al_o[...] = _dot(xa, a1_ref[...]).astype(al_o.dtype)


def _shift_lerp(x, gain, mod, layer, mu, w1, a1):
    s_len, d = x.shape
    tm = min(ROW_TM, s_len)
    row = pl.BlockSpec((tm, d), lambda i: (i, 0))
    low = pl.BlockSpec((tm, LORA_PAD), lambda i: (i, 0))
    down = pl.BlockSpec((d, LORA_PAD), lambda i: (0, 0))
    wide = jax.ShapeDtypeStruct((s_len, d), BF16)
    narrow = jax.ShapeDtypeStruct((s_len, LORA_PAD), BF16)
    return pl.pallas_call(
        functools.partial(_shift_lerp_kernel, tm=tm),
        out_shape=(wide, wide, wide, wide, narrow, narrow),
        grid=(s_len // tm,),
        in_specs=[
            row,
            pl.BlockSpec((8, d), lambda i: (jnp.maximum(i * (tm // 8) - 1, 0), 0)),
            pl.BlockSpec((1, d), lambda i: (0, 0)),
            pl.BlockSpec((1, 8, d), lambda i: (layer, 0, 1)),
            pl.BlockSpec((1, 8, d), lambda i: (layer, 0, 0)),
            pl.BlockSpec((6, d), lambda i: (0, 0)),
            down,
            down,
        ],
        out_specs=(row, row, row, row, low, low),
        compiler_params=_cparams("parallel"),
        name="rwkv_shift_lerp",
    )(x, x, gain.reshape(1, d), mod, mod, mu, w1, a1)


def _head_block_diag(value):
    shift = int(math.log2(RWKV_HEAD))
    same = (lax.shift_right_logical(_iota((LANES, LANES), 0), shift)
            == lax.shift_right_logical(_iota((LANES, LANES), 1), shift))
    return jnp.where(same, value, 0.0).astype(BF16)


def _head_sum(x, mat):
    hi = x.astype(BF16)
    lo = (x - hi.astype(F32)).astype(BF16)
    return _dot(hi, mat) + _dot(lo, mat)


def _rwkv_prep_kernel(k_ref, wa_ref, aa_ref, w2_ref, a2_ref, w0_ref, a0_ref, kk_ref, ka_ref, lw_o, kk_o, kka_o, k2_o):
    k = k_ref[...].astype(F32)
    lw_o[...] = -math.exp(-0.5) * jax.nn.sigmoid(w0_ref[...] + _dot(wa_ref[...], w2_ref[...]))
    a = jax.nn.sigmoid(a0_ref[...] + _dot(aa_ref[...], a2_ref[...]))
    kkr = k * kk_ref[...]
    ones = _head_block_diag(1.0)
    for j in range(k.shape[1] // LANES):
        sl = slice(j * LANES, (j + 1) * LANES)
        x = kkr[:, sl]
        kk = x / jnp.maximum(jnp.sqrt(_head_sum(x * x, ones)), 1e-12)
        kk_o[:, sl] = kk.astype(kk_o.dtype)
        kka_o[:, sl] = (kk * a[:, sl]).astype(kka_o.dtype)
    k2_o[...] = (k * (1.0 + (a - 1.0) * ka_ref[...])).astype(k2_o.dtype)


def _rwkv_prep(k, w_lora, a_lora, w2, a2, w0, a0, k_k, k_a):
    s_len, d = k.shape
    tm = min(ROW_TM, s_len)
    row = pl.BlockSpec((tm, d), lambda i: (i, 0))
    low = pl.BlockSpec((tm, LORA_PAD), lambda i: (i, 0))
    up = pl.BlockSpec((LORA_PAD, d), lambda i: (0, 0))
    vec = pl.BlockSpec((1, d), lambda i: (0, 0))
    return pl.pallas_call(
        _rwkv_prep_kernel,
        out_shape=(jax.ShapeDtypeStruct((s_len, d), F32),) + tuple(jax.ShapeDtypeStruct((s_len, d), BF16) for _ in range(3)),
        grid=(s_len // tm,),
        in_specs=[row, low, low, up, up, vec, vec, vec, vec],
        out_specs=(row, row, row, row),
        compiler_params=_cparams("parallel"),
        name="rwkv_prep",
    )(k, w_lora, a_lora, w2, a2, w0.reshape(1, d), a0.reshape(1, d), k_k.reshape(1, d), k_a.reshape(1, d))


def _rwkv_chunk_kernel(r_ref, lw_ref, k_ref, v_ref, kk_ref, kka_ref, y_ref, s_s, *, c, hb):
    n = RWKV_HEAD

    @pl.when(pl.program_id(1) == 0)
    def _():
        s_s[...] = jnp.zeros(s_s.shape, F32)

    rr = _iota((c, c), 0)
    cc = _iota((c, c), 1)
    lw = lw_ref[...]
    cum = _dot((rr >= cc).astype(F32), lw, HI)
    grow = jnp.exp(-cum)
    decay = jnp.exp(cum)
    total = decay[c - 1:c, :]
    q_t = kk_ref[...].astype(F32) * jnp.exp(cum - lw)
    p_t = -(kka_ref[...].astype(F32) * grow)
    k_t = k_ref[...].astype(F32) * grow
    r_t = r_ref[...].astype(F32) * decay
    p_end = p_t * total
    k_end = k_t * total
    r2 = _iota((2 * c, 2 * c), 0)
    c2 = _iota((2 * c, 2 * c), 1) & (c - 1)
    gram_mask = (((r2 < c) & (r2 > c2)) | ((r2 >= c) & ((r2 - c) >= c2))).astype(F32)
    eye = (rr == cc).astype(F32)
    same_block = [(lax.shift_right_logical(rr, s) == lax.shift_right_logical(cc, s)).astype(F32)
                  for s in range(3, int(math.log2(c)) + 1)]
    heads = range(hb)
    sls = [slice(i * n, (i + 1) * n) for i in heads]
    qr = [jnp.concatenate([q_t[:, sl], r_t[:, sl]], axis=0).astype(BF16) for sl in sls]
    pk = [jnp.concatenate([p_t[:, sl], k_t[:, sl]], axis=0).astype(BF16) for sl in sls]
    state = [s_s[i] for i in heads]
    from_state = [_dot_nt(qr[i], state[i].astype(BF16)) for i in heads]
    gram = [_dot_nt(qr[i], pk[i]) * gram_mask for i in heads]
    l_qp = [g[:c, :c] for g in gram]
    vb = [v_ref[:, sl] for sl in sls]
    from_v = [_dot(jnp.concatenate([gram[i][:c, c:], gram[i][c:, c:]], axis=0).astype(BF16), vb[i]) for i in heads]
    power = [l * same_block[0] for l in l_qp]
    inv = [eye + p for p in power]
    for _ in range(2):
        pb = [p.astype(BF16) for p in power]
        power = [_dot(p, p) for p in pb]
        inv = [x + _dot(x.astype(BF16), p.astype(BF16)) for x, p in zip(inv, power)]
    for lvl in range(1, len(same_block)):
        level_mask = same_block[lvl] - same_block[lvl - 1]
        ib = [x.astype(BF16) for x in inv]
        half = [_dot(ib[i], (l_qp[i] * level_mask).astype(BF16)).astype(BF16) for i in heads]
        inv = [inv[i] + _dot(half[i], ib[i]) for i in heads]
    u = [_dot(inv[i].astype(BF16), (from_state[i][:c] + from_v[i][:c]).astype(BF16)) for i in heads]
    ub = [x.astype(BF16) for x in u]
    for i in heads:
        y_ref[:, sls[i]] = from_state[i][c:] + _dot(gram[i][c:, :c].astype(BF16), ub[i]) + from_v[i][c:]
    for i in heads:
        uv = jnp.concatenate([ub[i], vb[i]], axis=0)
        pk_end = jnp.concatenate([p_end[:, sls[i]], k_end[:, sls[i]]], axis=0).astype(BF16)
        s_s[i] = state[i] * total[:, sls[i]] + _dot_tn(uv, pk_end)


def _rwkv_scan(r, lw, k2, v, kk, kka):
    s_len, d = r.shape
    c, hb = RWKV_CHUNK, RWKV_HB
    width = hb * RWKV_HEAD
    blk = pl.BlockSpec((c, width), lambda hg, ci: (ci, hg))
    return pl.pallas_call(
        functools.partial(_rwkv_chunk_kernel, c=c, hb=hb),
        out_shape=jax.ShapeDtypeStruct((s_len, d), F32),
        grid=(d // width, s_len // c),
        in_specs=[blk] * 6,
        out_specs=blk,
        scratch_shapes=[pltpu.VMEM((hb, RWKV_HEAD, RWKV_HEAD), F32)],
        compiler_params=_cparams("parallel", "arbitrary"),
        name="rwkv_chunk_scan",
    )(r, lw, k2, v, kk, kka)


def _rwkv_post_kernel(y_ref, r_ref, k2_ref, v_ref, z_ref, rk_ref, g_ref, b_ref, o_ref):
    avg = _head_block_diag(1.0 / RWKV_HEAD)
    ones = _head_block_diag(1.0)
    for j in range(y_ref.shape[1] // LANES):
        sl = slice(j * LANES, (j + 1) * LANES)
        y = y_ref[:, sl]
        dev = y - _head_sum(y, avg)
        var = _head_sum(dev * dev, avg)
        yn = dev * lax.rsqrt(var + LNX_EPS) * g_ref[:, sl] + b_ref[:, sl]
        rk = r_ref[:, sl].astype(F32) * k2_ref[:, sl].astype(F32) * rk_ref[:, sl]
        bonus = _head_sum(rk, ones) * v_ref[:, sl].astype(F32)
        o_ref[:, sl] = ((yn + bonus) * _silu(z_ref[:, sl].astype(F32))).astype(o_ref.dtype)


def _rwkv_post(y, r, k2, v, z, r_k, lnx_g, lnx_b):
    s_len, d = y.shape
    tm = min(ROW_TM, s_len)
    row = pl.BlockSpec((tm, d), lambda i: (i, 0))
    vec = pl.BlockSpec((1, d), lambda i: (0, 0))
    return pl.pallas_call(
        _rwkv_post_kernel,
        out_shape=jax.ShapeDtypeStruct((s_len, d), BF16),
        grid=(s_len // tm,),
        in_specs=[row, row, row, row, row, vec, vec, vec],
        out_specs=row,
        compiler_params=_cparams("parallel"),
        name="rwkv_post",
    )(y, r, k2, v, z, r_k.reshape(1, d), lnx_g.reshape(1, d), lnx_b.reshape(1, d))


def _rmsnorm_kernel(x_ref, g_ref, o_ref):
    x = x_ref[...].astype(F32)
    o_ref[...] = (x * lax.rsqrt(jnp.mean(x * x, axis=-1, keepdims=True) + EPS) * g_ref[...]).astype(o_ref.dtype)


def _rmsnorm(x, gain, out_dtype):
    s_len, d = x.shape
    tm = min(ROW_TM, s_len)
    return pl.pallas_call(
        _rmsnorm_kernel,
        out_shape=jax.ShapeDtypeStruct((s_len, d), out_dtype),
        grid=(s_len // tm,),
        in_specs=[pl.BlockSpec((tm, d), lambda i: (i, 0)), pl.BlockSpec((1, d), lambda i: (0, 0))],
        out_specs=pl.BlockSpec((tm, d), lambda i: (i, 0)),
        compiler_params=_cparams("parallel"),
        name="rmsnorm",
    )(x, gain.reshape(1, d))


def _rotate_half_cols(w):
    half = w.shape[1] // 2
    return jnp.concatenate([-w[:, half:], w[:, :half]], axis=1)


def _arrange_w_in(w_in):
    d = w_in.shape[0]
    q, kv, g, z, qa, kva, mlaz = jnp.split(w_in, [1024, 1792, 1840, 2864, 3376, 3952], axis=1)
    ckv, kpe = kva[:, :KV_LORA], kva[:, KV_LORA:]
    pad = jnp.zeros((d, IN0_COLS - C_G - g.shape[1]), w_in.dtype)
    return jnp.concatenate([q, z, mlaz, qa, ckv, kv, kpe, _rotate_half_cols(kpe), g, pad], axis=1).astype(BF16)


def _arrange_w_qb(w_qb):
    w = w_qb.reshape(Q_LORA, MLA_HEADS, QK_NOPE + QK_ROPE)
    nope = w[:, :, :QK_NOPE].reshape(Q_LORA, -1)
    pe = w[:, :, QK_NOPE:]
    half = QK_ROPE // 2
    rot = jnp.concatenate([-pe[:, :, half:], pe[:, :, :half]], axis=2)
    return jnp.concatenate([nope, pe.reshape(Q_LORA, -1), rot.reshape(Q_LORA, -1)], axis=1).astype(BF16)


def _arrange_w_kvb(w_kvb):
    w = w_kvb.reshape(KV_LORA, MLA_HEADS, QK_NOPE + V_HEAD)
    return jnp.concatenate([w[:, :, :QK_NOPE].reshape(KV_LORA, -1), w[:, :, QK_NOPE:].reshape(KV_LORA, -1)],
                           axis=1).astype(BF16)


def _rope_tables(s_len):
    inv = ROPE_THETA ** (-np.arange(0, QK_ROPE, 2, dtype=np.float32) / QK_ROPE)
    ang = np.arange(s_len, dtype=np.float32)[:, None] * inv[None].astype(np.float32)
    cos = np.concatenate([np.cos(ang), np.cos(ang)], axis=1).astype(np.float32)
    sin = np.concatenate([np.sin(ang), np.sin(ang)], axis=1).astype(np.float32)
    return jnp.asarray(cos), jnp.asarray(sin)


def _pad_cols(w, n):
    return jnp.pad(w, ((0, 0), (0, n - w.shape[1])))


def _pad_rows(w, n):
    return jnp.pad(w, ((0, n - w.shape[0]), (0, 0)))


def _attention_layer(x, mod, layer, gain, w_in, w_out, pe_k, w1_k, b1_k, w2_k, pe_v, w1_v, b1_v, w2_v,
                     qa_g, w_qb, kva_g, w_kvb):
    s_len, d = x.shape
    assert s_len % 1024 == 0 and s_len // SLC_LEN <= SEL_LANES
    n_cmp = (s_len - CMP_LEN) // CMP_STRIDE + 1
    n_slc = s_len // SLC_LEN
    proj = _matmul(x, _arrange_w_in(w_in), out_dtype=F32, prologue="rms_mod", gain=gain, mod=mod, layer=layer,
                   tm=IN_PROJ_TM, name="in_proj")
    qfull = _matmul(proj, _arrange_w_qb(w_qb), out_dtype=BF16, a_col=C_QA // Q_LORA, k=Q_LORA, prologue="rms",
                    gain=qa_g, name="mla_q_proj")
    kv = _matmul(proj, _arrange_w_kvb(w_kvb), out_dtype=BF16, a_col=C_CKV // KV_LORA, k=KV_LORA, prologue="rms",
                 gain=kva_g, name="mla_kv_proj")
    cos, sin = _rope_tables(s_len)
    y_mla = _mla_attention(qfull, kv, proj, cos, sin)
    cl = CMP_LEN * NSA_DK
    kcvc = proj[:, C_KV:C_KV + 2 * NSA_GROUPS * NSA_DK].reshape(s_len, 2, NSA_GROUPS, NSA_DK)
    r = kcvc.transpose(1, 2, 0, 3).reshape(2 * NSA_GROUPS, s_len // CMP_STRIDE, CMP_STRIDE * NSA_DK)
    cmp_tokens = _compress_tokens(
        r,
        jnp.stack([w1_k.reshape(cl, -1), w1_v.reshape(cl, -1)]),
        jnp.stack([pe_k.reshape(1, cl), pe_v.reshape(1, cl)]),
        jnp.stack([b1_k.reshape(1, -1), b1_v.reshape(1, -1)]),
        jnp.stack([w2_k, w2_v]),
        n_cmp)
    o_cmp, sel, block_any = _nsa_compressed(proj, cmp_tokens, n_cmp, n_slc)
    o_win = _nsa_flash(proj, "win")
    y_nsa = _nsa_flash(proj, "slc", sel=sel, o_cmp=o_cmp, o_win=o_win, block_any=block_any)
    return _matmul(y_nsa, w_out.astype(BF16), a2=y_mla, out_dtype=F32, epilogue="residual", res=x, mod=mod,
                   layer=layer, name="attn_out_proj")


def _rwkv_layer(x, mod, layer, gain, mu, w_r, w_k, w_v, w_z, w_o, w0, w1, w2, a0, a1, a2, k_k, k_a, r_k,
                lnx_g, lnx_b, final_gain=None):
    xr, xk, xv, xz, w_lora, a_lora = _shift_lerp(x, gain, mod, layer, mu, _pad_cols(w1, LORA_PAD).astype(BF16),
                                                 _pad_cols(a1, LORA_PAD).astype(BF16))
    r = _matmul(xr, w_r.astype(BF16), out_dtype=BF16, name="rwkv_r")
    k = _matmul(xk, w_k.astype(BF16), out_dtype=BF16, name="rwkv_k")
    v = _matmul(xv, w_v.astype(BF16), out_dtype=BF16, name="rwkv_v")
    z = _matmul(xz, w_z.astype(BF16), out_dtype=BF16, name="rwkv_z")
    lw, kk, kka, k2 = _rwkv_prep(k, w_lora, a_lora, _pad_rows(w2, LORA_PAD).astype(BF16),
                                 _pad_rows(a2, LORA_PAD).astype(BF16), w0, a0, k_k, k_a)
    y = _rwkv_scan(r, lw, k2, v, kk, kka)
    y = _rwkv_post(y, r, k2, v, z, r_k, lnx_g, lnx_b)
    epilogue = "residual" if final_gain is None else "residual_rms"
    return _matmul(y, w_o.astype(BF16), out_dtype=F32, epilogue=epilogue, res=x, mod=mod, layer=layer,
                   final_gain=final_gain, tm=OUT_RMS_TM if final_gain is not None else None, name="rwkv_out_proj")


def kernel(x, c, norm_g, ada_w, ada_b, final_g, a_w_in, a_w_out, nsa_pe_k, nsa_w1_k, nsa_b1_k, nsa_w2_k, nsa_pe_v, nsa_w1_v, nsa_b1_v, nsa_w2_v, mla_qa_g, mla_w_qb, mla_kva_g, mla_w_kvb, r_mu, r_w_r, r_w_k, r_w_v, r_w_z, r_w_o, r_w0, r_w1, r_w2, r_a0, r_a1, r_a2, r_k_k, r_k_a, r_r_k, r_lnx_g, r_lnx_b):
    batch, s_len, d = x.shape
    depth = ada_w.shape[0]
    outs = []
    for b in range(batch):
        mod = _modulation(c[b:b + 1], ada_w, ada_b)
        xb = x[b]
        for i in range(depth):
            j = i // 2
            if i % 2 == 0:
                xb = _attention_layer(xb, mod, i, norm_g[i], a_w_in[j], a_w_out[j],
                                      nsa_pe_k[j], nsa_w1_k[j], nsa_b1_k[j], nsa_w2_k[j],
                                      nsa_pe_v[j], nsa_w1_v[j], nsa_b1_v[j], nsa_w2_v[j],
                                      mla_qa_g[j], mla_w_qb[j], mla_kva_g[j], mla_w_kvb[j])
            else:
                xb = _rwkv_layer(xb, mod, i, norm_g[i], r_mu[j], r_w_r[j], r_w_k[j], r_w_v[j], r_w_z[j], r_w_o[j],
                                 r_w0[j], r_w1[j], r_w2[j], r_a0[j], r_a1[j], r_a2[j], r_k_k[j], r_k_a[j],
                                 r_r_k[j], r_lnx_g[j], r_lnx_b[j], final_gain=final_g if i == depth - 1 else None)
        outs.append(xb if depth % 2 == 0 else _rmsnorm(xb, final_g, x.dtype))
    return jnp.stack(outs)
```

```python
import functools
import math

import jax
import jax.numpy as jnp
import numpy as np
from jax import lax
from jax.experimental import pallas as pl
from jax.experimental.pallas import tpu as pltpu

F32 = jnp.float32
BF16 = jnp.bfloat16
HI = lax.Precision.HIGHEST

EPS = 1e-6
NEG_INF = -1e30
FORCE = 1e9
NSA_HEADS = 16
NSA_GROUPS = 2
NSA_HPG = NSA_HEADS // NSA_GROUPS
NSA_DK = 64
CMP_LEN = 32
CMP_STRIDE = 16
SLC_LEN = 64
SLC_TOPK = 16
WINDOW = 512
MLA_HEADS = 8
Q_LORA = 512
KV_LORA = 512
QK_NOPE = 128
QK_ROPE = 64
V_HEAD = 128
ROPE_THETA = 10000.0
RWKV_HEAD = 64
LORA_PAD = 128
LNX_EPS = 64e-5

LANES = 128
VMEM_LIMIT_BYTES = 48 * 1024 * 1024

MM_TM = 1024
MM_TM_F32 = 512
MM_TN = 1024
IN_PROJ_TM = 1024
OUT_RMS_TM = 512
ROW_TM = 256
MOD_TN = 1536
MLA_T = 512
NSA_TQ = 128
NSA_TK = 512
NSA_SLOTS = 4
RWKV_CHUNK = 64
RWKV_HB = 32
SEL_LANES = 128
SM_ROWS = 64

LOG2E = 1.0 / math.log(2.0)
M_FLOOR = -1e20

C_Q = 0
C_Z = 1024
C_MLAZ = 2048
C_QA = 3072
C_CKV = 3584
C_KV = 4096
C_KPE = 4864
C_G = 4992
IN0_COLS = 5120


def _cparams(*sem):
    return pltpu.CompilerParams(dimension_semantics=sem, vmem_limit_bytes=VMEM_LIMIT_BYTES)


def _dot(a, b, precision=None):
    return jnp.dot(a, b, preferred_element_type=F32, precision=precision)


def _dot_nt(a, b):
    return lax.dot_general(a, b, (((1,), (1,)), ((), ())), preferred_element_type=F32)


def _dot_tn(a, b):
    return lax.dot_general(a, b, (((0,), (0,)), ((), ())), preferred_element_type=F32)


def _silu(x):
    return x * jax.nn.sigmoid(x)


def _iota(shape, dim):
    return lax.broadcasted_iota(jnp.int32, shape, dim)


def _lane_tile_reduce(x, op):
    parts = [x[:, i * LANES:(i + 1) * LANES] for i in range(x.shape[1] // LANES)]
    while len(parts) > 1:
        parts = [op(parts[i], parts[i + 1]) if i + 1 < len(parts) else parts[i] for i in range(0, len(parts), 2)]
    return parts[0]


def _mod_kernel(c_ref, w_ref, b_ref, o_ref):
    c = c_ref[...]
    sc = jnp.broadcast_to(_silu(c), (8, c.shape[1]))
    o_ref[0] = _dot(sc, w_ref[0], HI) + b_ref[0]


def _modulation(c, ada_w, ada_b):
    depth, d, n = ada_w.shape
    tn = MOD_TN
    return pl.pallas_call(
        _mod_kernel,
        out_shape=jax.ShapeDtypeStruct((depth, 8, n), F32),
        grid=(depth, n // tn),
        in_specs=[
            pl.BlockSpec((1, d), lambda i, j: (0, 0)),
            pl.BlockSpec((1, d, tn), lambda i, j: (i, 0, j)),
            pl.BlockSpec((1, 1, tn), lambda i, j: (i, 0, j)),
        ],
        out_specs=pl.BlockSpec((1, 8, tn), lambda i, j: (i, 0, j)),
        compiler_params=_cparams("parallel", "parallel"),
        name="adaln_mod",
    )(c, ada_w, ada_b.reshape(depth, 1, n))


def _mm_kernel(*refs, prologue, epilogue, split_k):
    it = iter(refs)
    a_ref = next(it)
    a2_ref = next(it) if split_k else None
    b_ref = next(it)
    g_ref = next(it) if prologue != "none" else None
    sc_ref, sh_ref = (next(it), next(it)) if prologue == "rms_mod" else (None, None)
    res_ref, gate_ref = (next(it), next(it)) if epilogue.startswith("residual") else (None, None)
    fg_ref = next(it) if epilogue == "residual_rms" else None
    o_ref = next(it)
    if prologue == "none":
        a = a_ref[...]
    else:
        a_s = next(it)

        @pl.when(pl.program_id(1) == 0)
        def _():
            x = a_ref[...].astype(F32)
            y = x * lax.rsqrt(jnp.mean(x * x, axis=-1, keepdims=True) + EPS) * g_ref[...]
            if prologue == "rms_mod":
                y = y * (1.0 + sc_ref[0, 0:1, :]) + sh_ref[0, 0:1, :]
            a_s[...] = y.astype(BF16)

        a = a_s[...]
    if split_k:
        k1 = a.shape[1]
        acc = _dot(a, b_ref[0:k1, :]) + _dot(a2_ref[...], b_ref[k1:, :])
    else:
        acc = _dot(a, b_ref[...])
    if epilogue.startswith("residual"):
        acc = res_ref[...] + gate_ref[0, 0:1, :] * acc
        if epilogue == "residual_rms":
            acc = acc * lax.rsqrt(jnp.mean(acc * acc, axis=-1, keepdims=True) + EPS) * fg_ref[...]
    o_ref[...] = acc.astype(o_ref.dtype)


def _matmul(a, b, *, out_dtype, a2=None, a_col=0, k=None, prologue="none", gain=None, mod=None, layer=0,
            epilogue="none", res=None, final_gain=None, tm=None, tn=None, name="matmul"):
    m = a.shape[0]
    n = b.shape[1]
    k = (b.shape[0] if a2 is None else a.shape[1]) if k is None else k
    d_mod = None if mod is None else mod.shape[2] // 3
    tm = min(tm or (MM_TM if prologue == "none" else MM_TM_F32), m)
    tn = n if epilogue == "residual_rms" else min(tn or MM_TN, n)
    in_specs = [pl.BlockSpec((tm, k), lambda i, j: (i, a_col))]
    args = [a]
    if a2 is not None:
        in_specs.append(pl.BlockSpec((tm, a2.shape[1]), lambda i, j: (i, 0)))
        args.append(a2)
    in_specs.append(pl.BlockSpec((b.shape[0], tn), lambda i, j: (0, j)))
    args.append(b)
    scratch = []
    if prologue != "none":
        in_specs.append(pl.BlockSpec((1, k), lambda i, j: (0, 0)))
        args.append(gain.reshape(1, k).astype(F32))
        scratch.append(pltpu.VMEM((tm, k), BF16))
    if prologue == "rms_mod":
        in_specs += [pl.BlockSpec((1, 8, k), lambda i, j: (layer, 0, 1)), pl.BlockSpec((1, 8, k), lambda i, j: (layer, 0, 0))]
        args += [mod, mod]
    if epilogue.startswith("residual"):
        in_specs += [pl.BlockSpec((tm, tn), lambda i, j: (i, j)),
                     pl.BlockSpec((1, 8, tn), lambda i, j: (layer, 0, 2 * (d_mod // tn) + j))]
        args += [res, mod]
    if epilogue == "residual_rms":
        in_specs.append(pl.BlockSpec((1, n), lambda i, j: (0, 0)))
        args.append(final_gain.reshape(1, n))
    return pl.pallas_call(
        functools.partial(_mm_kernel, prologue=prologue, epilogue=epilogue, split_k=a2 is not None),
        out_shape=jax.ShapeDtypeStruct((m, n), out_dtype),
        grid=(m // tm, n // tn),
        in_specs=in_specs,
        out_specs=pl.BlockSpec((tm, tn), lambda i, j: (i, j)),
        scratch_shapes=scratch,
        compiler_params=_cparams("parallel", "arbitrary"),
        name=name,
    )(*args)


def _mla_attn_kernel(qt_ref, kt_ref, qn_ref, qp_ref, qpr_ref, cq_ref, sq_ref, k_ref, v_ref, kpe_ref, ck_ref, sk_ref, z_ref,
                     o_ref, q_s, kc_s, s_s, p_s, a_s, m_s, l_s, acc_s, *, t, scale):
    qi = qt_ref[pl.program_id(0)]
    ki = kt_ref[pl.program_id(0)]
    dq = QK_NOPE + QK_ROPE

    @pl.when(ki == 0)
    def _():
        qn = (qn_ref[...].astype(F32) * (scale * LOG2E)).astype(BF16)
        cos = cq_ref[...] * (scale * LOG2E)
        sin = sq_ref[...] * (scale * LOG2E)
        for h in range(MLA_HEADS):
            pe = slice(h * QK_ROPE, (h + 1) * QK_ROPE)
            qp = qp_ref[:, pe].astype(F32) * cos + qpr_ref[:, pe].astype(F32) * sin
            q_s[:, h * dq:h * dq + QK_NOPE] = qn[:, h * QK_NOPE:(h + 1) * QK_NOPE]
            q_s[:, h * dq + QK_NOPE:(h + 1) * dq] = qp.astype(BF16)
        m_s[...] = jnp.full(m_s.shape, M_FLOOR, F32)
        l_s[...] = jnp.zeros(l_s.shape, F32)
        acc_s[...] = jnp.zeros(acc_s.shape, F32)

    def step(diagonal):
        kpe = kpe_ref[...]
        kp = (kpe[:, :QK_ROPE] * ck_ref[...] + kpe[:, QK_ROPE:] * sk_ref[...]).astype(BF16)
        for h in range(MLA_HEADS):
            kc_s[h, :, 0:QK_NOPE] = k_ref[:, h * QK_NOPE:(h + 1) * QK_NOPE]
            kc_s[h, :, QK_NOPE:dq] = kp
        for h in range(MLA_HEADS):
            s = _dot_nt(q_s[:, h * dq:(h + 1) * dq], kc_s[h])
            if diagonal:
                s = jnp.where(_iota((t, 1), 0) >= _iota((1, t), 1), s, NEG_INF)
            s_s[h] = s
            row_max = jnp.max(_lane_tile_reduce(s, jnp.maximum), axis=1, keepdims=True)
            m_prev = m_s[h]
            m_new = jnp.maximum(m_prev, jnp.broadcast_to(row_max, (t, LANES)))
            a_s[h] = jnp.exp2(m_prev - m_new)
            m_s[h] = m_new
        for h in range(MLA_HEADS):
            for rb in range(t // SM_ROWS):
                r = slice(rb * SM_ROWS, (rb + 1) * SM_ROWS)
                p = jnp.exp2(s_s[h, r] - jnp.tile(m_s[h, r], (1, t // LANES)))
                p_s[h, r] = p.astype(BF16)
                row_sum = jnp.sum(_lane_tile_reduce(p, jnp.add), axis=1, keepdims=True)
                l_s[h, r] = a_s[h, r] * l_s[h, r] + jnp.broadcast_to(row_sum, (SM_ROWS, LANES))
        for h in range(MLA_HEADS):
            acc_s[h] = a_s[h] * acc_s[h] + _dot(p_s[h], v_ref[:, h * V_HEAD:(h + 1) * V_HEAD])

    @pl.when(ki < qi)
    def _():
        step(False)

    @pl.when(ki == qi)
    def _():
        step(True)
        for h in range(MLA_HEADS):
            sl = slice(h * V_HEAD, (h + 1) * V_HEAD)
            o_ref[:, sl] = (acc_s[h] / l_s[h] * _silu(z_ref[:, sl])).astype(o_ref.dtype)


def _mla_attention(qfull, kv, proj, cos, sin):
    s_len = qfull.shape[0]
    t = min(MLA_T, s_len)
    n = s_len // t
    hn = MLA_HEADS * QK_NOPE
    hp = MLA_HEADS * QK_ROPE
    hv = MLA_HEADS * V_HEAD
    pairs = [(qi, ki) for qi in range(n) for ki in range(qi + 1)]
    q_tiles = jnp.asarray([p[0] for p in pairs], jnp.int32)
    k_tiles = jnp.asarray([p[1] for p in pairs], jnp.int32)
    q_blk = lambda col: (lambda i, qt, kt: (qt[i], col))
    k_blk = lambda col: (lambda i, qt, kt: (kt[i], col))
    grid_spec = pltpu.PrefetchScalarGridSpec(
        num_scalar_prefetch=2,
        grid=(len(pairs),),
        in_specs=[
            pl.BlockSpec((t, hn), q_blk(0)),
            pl.BlockSpec((t, hp), q_blk(hn // hp)),
            pl.BlockSpec((t, hp), q_blk(hn // hp + 1)),
            pl.BlockSpec((t, QK_ROPE), q_blk(0)),
            pl.BlockSpec((t, QK_ROPE), q_blk(0)),
            pl.BlockSpec((t, hn), k_blk(0)),
            pl.BlockSpec((t, hv), k_blk(1)),
            pl.BlockSpec((t, 2 * QK_ROPE), k_blk(C_KPE // (2 * QK_ROPE))),
            pl.BlockSpec((t, QK_ROPE), k_blk(0)),
            pl.BlockSpec((t, QK_ROPE), k_blk(0)),
            pl.BlockSpec((t, hv), q_blk(C_MLAZ // hv)),
        ],
        out_specs=pl.BlockSpec((t, hv), q_blk(0)),
        scratch_shapes=[
            pltpu.VMEM((t, hn + hp), BF16),
            pltpu.VMEM((MLA_HEADS, t, QK_NOPE + QK_ROPE), BF16),
            pltpu.VMEM((MLA_HEADS, t, t), F32),
            pltpu.VMEM((MLA_HEADS, t, t), BF16),
            pltpu.VMEM((MLA_HEADS, t, LANES), F32),
            pltpu.VMEM((MLA_HEADS, t, LANES), F32),
            pltpu.VMEM((MLA_HEADS, t, LANES), F32),
            pltpu.VMEM((MLA_HEADS, t, V_HEAD), F32),
        ],
    )
    return pl.pallas_call(
        functools.partial(_mla_attn_kernel, t=t, scale=(QK_NOPE + QK_ROPE) ** -0.5),
        out_shape=jax.ShapeDtypeStruct((s_len, hv), BF16),
        grid_spec=grid_spec,
        compiler_params=_cparams("arbitrary"),
        name="mla_attention",
    )(q_tiles, k_tiles, qfull, qfull, qfull, cos, sin, kv, kv, proj, cos, sin, proj)


def _cmp_tokens_kernel(r_ref, w1_ref, pe_ref, b1_ref, w2_ref, o_ref, *, n_cmp):
    r = r_ref[0]
    w1 = w1_ref[0]
    half = w1.shape[0] // 2
    first = _dot(r, w1[:half], HI)
    second = _dot(r, w1[half:], HI)
    rows = r.shape[0]
    second = pltpu.roll(second, rows - 1, 0)
    hid = first + second + _dot(pe_ref[0], w1, HI) + b1_ref[0]
    out = _dot(_silu(hid), w2_ref[0], HI)
    o_ref[0] = jnp.where(_iota((rows, 1), 0) < n_cmp, out, 0.0)


def _compress_tokens(r, w1, pe, b1, w2, n_cmp):
    four, rows, width = r.shape
    hid = w1.shape[2]
    dk = w2.shape[2]
    return pl.pallas_call(
        functools.partial(_cmp_tokens_kernel, n_cmp=n_cmp),
        out_shape=jax.ShapeDtypeStruct((four, rows, dk), F32),
        grid=(four,),
        in_specs=[
            pl.BlockSpec((1, rows, width), lambda i: (i, 0, 0)),
            pl.BlockSpec((1, 2 * width, hid), lambda i: (i // NSA_GROUPS, 0, 0)),
            pl.BlockSpec((1, 1, 2 * width), lambda i: (i // NSA_GROUPS, 0, 0)),
            pl.BlockSpec((1, 1, hid), lambda i: (i // NSA_GROUPS, 0, 0)),
            pl.BlockSpec((1, hid, dk), lambda i: (i // NSA_GROUPS, 0, 0)),
        ],
        out_specs=pl.BlockSpec((1, rows, dk), lambda i: (i, 0, 0)),
        compiler_params=_cparams("parallel"),
        name="nsa_compress_tokens",
    )(r, w1, pe, b1, w2)


def _alibi_slope(h):
    return 2.0 ** (-8.0 * (h + 1) / NSA_HEADS)


def _bf16_parts(x, n=3):
    parts = []
    for _ in range(n):
        hi = float(np.asarray(x, dtype=BF16).astype(np.float32))
        parts.append(hi)
        x = x - hi
    return parts


def _nsa_cmp_kernel(q_ref, g_ref, cmp_ref, o_ref, sel_ref, any_ref, qa_s, ka_s, va_s, s_s, m_s, ehi_s, elo_s,
                    *, tq, n_cmp, n_slc):
    qi = pl.program_id(0)
    n_pad = cmp_ref.shape[1]
    t_col = qi * tq + _iota((tq, 1), 0)
    n_row = _iota((1, n_pad), 1)
    cmp_end = n_row * CMP_STRIDE + (CMP_LEN - 1)
    bias = jnp.where((cmp_end <= t_col) & (n_row < n_cmp), 0.0, NEG_INF)
    gates = jax.nn.sigmoid(g_ref[...])
    lane = _iota((n_pad, NSA_DK), 1)
    pos = _iota((n_pad, NSA_DK), 0) * CMP_STRIDE + (CMP_LEN - 1)
    upper = (pos & ~255).astype(F32)
    lower = (pos & 255).astype(F32)
    feat = jnp.where(lane < 3, upper, jnp.where(lane < 6, lower, 0.0)).astype(BF16)
    ones_col = jnp.where(lane == 0, 1.0, 0.0).astype(BF16)
    for g in range(NSA_GROUPS):
        ka_s[g, :, 0:NSA_DK] = cmp_ref[g].astype(BF16)
        ka_s[g, :, NSA_DK:2 * NSA_DK] = feat
        va_s[g, :, 0:NSA_DK] = cmp_ref[NSA_GROUPS + g].astype(BF16)
        va_s[g, :, NSA_DK:2 * NSA_DK] = ones_col
    qlane = _iota((1, NSA_DK), 1)
    for h in range(NSA_HEADS):
        qa_s[h, :, 0:NSA_DK] = (q_ref[:, h * NSA_DK:(h + 1) * NSA_DK] * (NSA_DK ** -0.5 * LOG2E)).astype(BF16)
        qfeat = jnp.zeros((1, NSA_DK), F32)
        for i, part in enumerate(2 * _bf16_parts(LOG2E * _alibi_slope(h))):
            qfeat = jnp.where(qlane == i, part, qfeat)
        qa_s[h, :, NSA_DK:2 * NSA_DK] = jnp.broadcast_to(qfeat, (tq, NSA_DK)).astype(BF16)
    n_c = _iota((n_pad, SEL_LANES), 0) * CMP_STRIDE
    j_c = _iota((n_pad, SEL_LANES), 1) * SLC_LEN
    overlap = ((n_c < j_c + SLC_LEN) & (n_c + CMP_LEN > j_c)).astype(BF16)
    j_row = _iota((1, SEL_LANES), 1)
    cur = lax.shift_right_logical(t_col, int(math.log2(SLC_LEN)))
    forced = (j_row == 0) | (j_row == cur) | (j_row == cur - 1)
    causal = (j_row * SLC_LEN <= t_col) & (j_row < n_slc)
    j_t = _iota((SEL_LANES, tq), 0)
    for h in range(NSA_HEADS):
        s = _dot_nt(qa_s[h], ka_s[h // NSA_HPG]) + bias
        s_s[h] = s
        row_max = jnp.max(_lane_tile_reduce(s, jnp.maximum), axis=1, keepdims=True)
        m_s[h] = jnp.maximum(jnp.broadcast_to(row_max, (tq, LANES)), M_FLOOR)
    for h in range(NSA_HEADS):
        e = jnp.exp2(s_s[h] - jnp.tile(m_s[h], (1, n_pad // LANES)))
        e_hi = e.astype(BF16)
        ehi_s[h] = e_hi
        elo_s[h] = (e - e_hi.astype(F32)).astype(BF16)
    for g in range(NSA_GROUPS):
        imp = jnp.zeros((tq, SEL_LANES), F32)
        for hh in range(NSA_HPG):
            h = g * NSA_HPG + hh
            sl = slice(h * NSA_DK, (h + 1) * NSA_DK)
            pv = _dot(ehi_s[h], va_s[g]) + _dot(elo_s[h], va_s[g])
            den = pv[:, NSA_DK:NSA_DK + 1]
            inv_den = 1.0 / jnp.where(den > 0.0, den, 1.0)
            o_ref[:, sl] = pv[:, 0:NSA_DK] * (inv_den * gates[:, 3 * h:3 * h + 1])
            imp = imp + (_dot(ehi_s[h], overlap) + _dot(elo_s[h], overlap)) * inv_den
        imp = jnp.where(forced, FORCE, imp)
        imp = jnp.where(causal, imp, jnp.where(j_row < n_slc, NEG_INF, -3e38))
        imp_t = imp.T
        sel_t = jnp.zeros((SEL_LANES, tq), F32)
        for _ in range(min(SLC_TOPK, n_slc)):
            best = jnp.max(imp_t, axis=0, keepdims=True)
            first = jnp.min(jnp.where(imp_t == best, j_t, SEL_LANES), axis=0, keepdims=True)
            hit = j_t == first
            sel_t = jnp.where(hit, 1.0, sel_t)
            imp_t = jnp.where(hit, -3.4e38, imp_t)
        sel = sel_t.T
        sel_ref[:, g * SEL_LANES:(g + 1) * SEL_LANES] = sel.astype(sel_ref.dtype)
        tile_any = jnp.max(sel, axis=0, keepdims=True)
        picked_any = tile_any if g == 0 else jnp.maximum(picked_any, tile_any)
    any_ref[0] = jnp.broadcast_to(picked_any, (8, SEL_LANES))


def _nsa_compressed(proj, cmp_tokens, n_cmp, n_slc):
    s_len = proj.shape[0]
    tq = NSA_TQ
    hd = NSA_HEADS * NSA_DK
    return pl.pallas_call(
        functools.partial(_nsa_cmp_kernel, tq=tq, n_cmp=n_cmp, n_slc=n_slc),
        out_shape=(jax.ShapeDtypeStruct((s_len, hd), F32),
                   jax.ShapeDtypeStruct((s_len, NSA_GROUPS * SEL_LANES), BF16),
                   jax.ShapeDtypeStruct((s_len // tq, 8, SEL_LANES), F32)),
        grid=(s_len // tq,),
        in_specs=[
            pl.BlockSpec((tq, hd), lambda i: (i, C_Q // hd)),
            pl.BlockSpec((tq, LANES), lambda i: (i, C_G // LANES)),
            pl.BlockSpec(cmp_tokens.shape, lambda i: (0, 0, 0)),
        ],
        out_specs=(pl.BlockSpec((tq, hd), lambda i: (i, 0)),
                   pl.BlockSpec((tq, NSA_GROUPS * SEL_LANES), lambda i: (i, 0)),
                   pl.BlockSpec((1, 8, SEL_LANES), lambda i: (i, 0, 0))),
        scratch_shapes=[
            pltpu.VMEM((NSA_HEADS, tq, 2 * NSA_DK), BF16),
            pltpu.VMEM((NSA_GROUPS, cmp_tokens.shape[1], 2 * NSA_DK), BF16),
            pltpu.VMEM((NSA_GROUPS, cmp_tokens.shape[1], 2 * NSA_DK), BF16),
            pltpu.VMEM((NSA_HEADS, tq, cmp_tokens.shape[1]), F32),
            pltpu.VMEM((NSA_HEADS, tq, LANES), F32),
            pltpu.VMEM((NSA_HEADS, tq, cmp_tokens.shape[1]), BF16),
            pltpu.VMEM((NSA_HEADS, tq, cmp_tokens.shape[1]), BF16),
        ],
        compiler_params=_cparams("parallel"),
        name="nsa_compressed_topk",
    )(proj, proj, cmp_tokens)


def _nsa_tile_range(mode, qi, tq, tk):
    q0 = qi * tq
    last = lax.div(q0 + tq - 1, tk)
    if mode == "slc":
        return jnp.zeros_like(last), last
    return lax.div(jnp.maximum(q0 - (WINDOW - 1), 0), tk), last


def _nsa_flash_kernel(*refs, mode, tq, tk, branch, slots):
    if mode == "slc":
        tiles_ref, count_ref = refs[:2]
        q_ref, g_ref = refs[2:4]
        k_refs = refs[4:4 + slots]
        v_refs = refs[4 + slots:4 + 2 * slots]
        sel_ref, ocmp_ref, owin_ref, z_ref, o_ref, q_s, s_s, p_s, a_s, m_s, acc_s = refs[4 + 2 * slots:]
    else:
        q_ref, g_ref, k_ref, v_ref, o_ref, q_s, s_s, p_s, a_s, m_s, acc_s = refs
    qi = pl.program_id(0)
    step = pl.program_id(1)
    rows = NSA_HPG * tq
    feat_lane = [NSA_DK * (1 - g) for g in range(NSA_GROUPS)]

    @pl.when(step == 0)
    def _():
        lane = _iota((1, NSA_DK), 1)
        for h in range(NSA_HEADS):
            g, hh = divmod(h, NSA_HPG)
            r = slice(hh * tq, (hh + 1) * tq)
            q = q_ref[:, h * NSA_DK:(h + 1) * NSA_DK] * (NSA_DK ** -0.5 * LOG2E)
            q_s[g, r, g * NSA_DK:(g + 1) * NSA_DK] = q.astype(BF16)
            feat = jnp.zeros((1, NSA_DK), F32)
            for i, part in enumerate(3 * _bf16_parts(LOG2E * _alibi_slope(h))):
                feat = jnp.where(lane == i, part, feat)
            q_s[g, r, feat_lane[g]:feat_lane[g] + NSA_DK] = jnp.broadcast_to(feat, (tq, NSA_DK)).astype(BF16)
        m_s[...] = jnp.full(m_s.shape, M_FLOOR, F32)
        acc_s[...] = jnp.zeros(acc_s.shape, F32)

    def process(k_ref, v_ref, kt):
        k0 = kt * tk
        delta = (qi * tq + _iota((tq, 1), 0)) - (k0 + _iota((1, tk), 1))
        if mode == "slc":
            blk = lax.div(k0, SLC_LEN) + lax.shift_right_logical(_iota((SEL_LANES, tk), 1), int(math.log2(SLC_LEN)))
            expand = (_iota((SEL_LANES, tk), 0) == blk).astype(BF16)
        else:
            bias = jnp.where((delta >= 0) & (delta < WINDOW), 0.0, NEG_INF)
        jj = _iota((tk, LANES), 0)
        lane = _iota((tk, LANES), 1)
        tile_off = (k0 - qi * tq).astype(F32)
        upper = jnp.where(jj >= 256, 256.0, 0.0)
        lower = (jj & 255).astype(F32)
        kblk = k_ref[...]
        vblk = v_ref[...]
        for g in range(NSA_GROUPS):
            if mode == "slc":
                picked = _dot(sel_ref[:, g * SEL_LANES:(g + 1) * SEL_LANES], expand)
                bias = jnp.where((picked > 0.5) & (delta >= 0), 0.0, NEG_INF)
            f = lane - feat_lane[g]
            data = lax.shift_right_logical(lane, int(math.log2(NSA_DK))) == g
            k_aug = jnp.where(data, kblk, 0.0)
            k_aug = jnp.where((f >= 0) & (f < 3), tile_off, k_aug)
            k_aug = jnp.where((f >= 3) & (f < 6), upper, k_aug)
            k_aug = jnp.where((f >= 6) & (f < 9), lower, k_aug)
            s = _dot_nt(q_s[g], k_aug.astype(BF16))
            for hh in range(NSA_HPG):
                r = slice(hh * tq, (hh + 1) * tq)
                sb = s[r] + bias
                s_s[g, r] = sb
                row_max = jnp.max(_lane_tile_reduce(sb, jnp.maximum), axis=1, keepdims=True)
                m_prev = m_s[g, r]
                m_new = jnp.maximum(m_prev, jnp.broadcast_to(row_max, (tq, LANES)))
                a_s[g, r] = jnp.exp2(m_prev - m_new)
                m_s[g, r] = m_new
        for g in range(NSA_GROUPS):
            for rb in range(rows // SM_ROWS):
                r = slice(rb * SM_ROWS, (rb + 1) * SM_ROWS)
                p_s[g, r] = jnp.exp2(s_s[g, r] - jnp.tile(m_s[g, r], (1, tk // LANES))).astype(BF16)
        for g in range(NSA_GROUPS):
            data = lax.shift_right_logical(lane, int(math.log2(NSA_DK))) == g
            v_aug = jnp.where(lane == feat_lane[g], 1.0, jnp.where(data, vblk, 0.0)).astype(BF16)
            acc_s[g] = a_s[g] * acc_s[g] + _dot(p_s[g], v_aug)

    if mode == "slc":
        per_tile = pl.num_programs(1) * slots
        for slot in range(slots):
            entry = step * slots + slot

            @pl.when(entry < count_ref[qi])
            def _(slot=slot, entry=entry):
                process(k_refs[slot], v_refs[slot], tiles_ref[qi * per_tile + entry])
    else:
        lo, hi = _nsa_tile_range(mode, qi, tq, tk)

        @pl.when(lo + step <= hi)
        def _():
            process(k_ref, v_ref, lo + step)

    @pl.when(step == pl.num_programs(1) - 1)
    def _():
        hd = NSA_HEADS * NSA_DK
        pick_gate = (_iota((LANES, hd), 0)
                     == 3 * lax.shift_right_logical(_iota((LANES, hd), 1), int(math.log2(NSA_DK))) + branch).astype(BF16)
        gate_all = _head_sum(jax.nn.sigmoid(g_ref[...]), pick_gate)
        for h in range(NSA_HEADS):
            g, hh = divmod(h, NSA_HPG)
            r = slice(hh * tq, (hh + 1) * tq)
            sl = slice(h * NSA_DK, (h + 1) * NSA_DK)
            acc = acc_s[g, r]
            pick_total = (_iota((LANES, NSA_DK), 0) == feat_lane[g]).astype(BF16)
            o = acc[:, g * NSA_DK:(g + 1) * NSA_DK] / _head_sum(acc, pick_total) * gate_all[:, sl]
            if mode == "slc":
                o = (o + ocmp_ref[:, sl] + owin_ref[:, sl]) * _silu(z_ref[:, sl])
            o_ref[:, sl] = o.astype(o_ref.dtype)


def _nsa_flash(proj, mode, sel=None, o_cmp=None, o_win=None, block_any=None):
    s_len = proj.shape[0]
    tq, tk = NSA_TQ, min(NSA_TK, s_len)
    assert tk % LANES == 0 and tk <= 512
    hd = NSA_HEADS * NSA_DK
    gd = NSA_GROUPS * NSA_DK
    k_slab, v_slab, branch = (2, 3, 1) if mode == "slc" else (4, 5, 2)
    nq = s_len // tq
    nsteps = s_len // tk if mode == "slc" else -(-(WINDOW - 1) // tk) + 1
    rows = NSA_HPG * tq
    scratch = [
        pltpu.VMEM((NSA_GROUPS, rows, LANES), BF16),
        pltpu.VMEM((NSA_GROUPS, rows, tk), F32),
        pltpu.VMEM((NSA_GROUPS, rows, tk), BF16),
        pltpu.VMEM((NSA_GROUPS, rows, LANES), F32),
        pltpu.VMEM((NSA_GROUPS, rows, LANES), F32),
        pltpu.VMEM((NSA_GROUPS, rows, LANES), F32),
    ]
    out_shape = jax.ShapeDtypeStruct((s_len, hd), BF16 if mode == "slc" else F32)
    if mode == "win":
        def kv_map(slab):
            def index(qi, step):
                lo, hi = _nsa_tile_range(mode, qi, tq, tk)
                return jnp.minimum(lo + step, hi), C_KV // gd + slab
            return index

        q_map = lambda qi, step: (qi, 0)
        return pl.pallas_call(
            functools.partial(_nsa_flash_kernel, mode=mode, tq=tq, tk=tk, branch=branch, slots=1),
            out_shape=out_shape, grid=(nq, nsteps),
            in_specs=[
                pl.BlockSpec((tq, hd), lambda qi, step: (qi, C_Q // hd)),
                pl.BlockSpec((tq, LANES), lambda qi, step: (qi, C_G // LANES)),
                pl.BlockSpec((tk, gd), kv_map(k_slab)),
                pl.BlockSpec((tk, gd), kv_map(v_slab)),
            ],
            out_specs=pl.BlockSpec((tq, hd), q_map),
            scratch_shapes=scratch,
            compiler_params=_cparams("parallel", "arbitrary"),
            name="nsa_window",
        )(proj, proj, proj, proj)
    ntiles = s_len // tk
    slots = min(NSA_SLOTS, ntiles)
    blocks_per_tile = tk // SLC_LEN
    used = block_any[:, 0, :ntiles * blocks_per_tile].reshape(nq, ntiles, blocks_per_tile).max(axis=-1) > 0.5
    tile_ids = jnp.arange(ntiles, dtype=jnp.int32)
    causal = tile_ids[None, :] * tk <= (jnp.arange(nq, dtype=jnp.int32)[:, None] * tq + tq - 1)
    visit = used & causal
    count = jnp.maximum(visit.sum(axis=1).astype(jnp.int32), 1)
    order = jnp.argsort(jnp.where(visit, tile_ids[None, :], ntiles + tile_ids[None, :]), axis=1).astype(jnp.int32)
    tiles = jnp.take_along_axis(order, jnp.minimum(tile_ids[None, :], count[:, None] - 1), axis=1).reshape(-1)

    def kv_map(slab, slot):
        def index(qi, step, tiles_ref, count_ref):
            return tiles_ref[qi * ntiles + step * slots + slot], C_KV // gd + slab
        return index

    q_map = lambda qi, step, t, c: (qi, 0)
    grid_spec = pltpu.PrefetchScalarGridSpec(
        num_scalar_prefetch=2,
        grid=(nq, ntiles // slots),
        in_specs=[
            pl.BlockSpec((tq, hd), lambda qi, step, t, c: (qi, C_Q // hd)),
            pl.BlockSpec((tq, LANES), lambda qi, step, t, c: (qi, C_G // LANES)),
            *[pl.BlockSpec((tk, gd), kv_map(k_slab, slot)) for slot in range(slots)],
            *[pl.BlockSpec((tk, gd), kv_map(v_slab, slot)) for slot in range(slots)],
            pl.BlockSpec((tq, NSA_GROUPS * SEL_LANES), q_map),
            pl.BlockSpec((tq, hd), q_map),
            pl.BlockSpec((tq, hd), q_map),
            pl.BlockSpec((tq, hd), lambda qi, step, t, c: (qi, C_Z // hd)),
        ],
        out_specs=pl.BlockSpec((tq, hd), q_map),
        scratch_shapes=scratch,
    )
    return pl.pallas_call(
        functools.partial(_nsa_flash_kernel, mode=mode, tq=tq, tk=tk, branch=branch, slots=slots),
        out_shape=out_shape, grid_spec=grid_spec,
        compiler_params=_cparams("parallel", "arbitrary"),
        name="nsa_selected",
    )(tiles, count, proj, proj, *([proj] * (2 * slots)), sel, o_cmp, o_win, proj)


def _shift_lerp_kernel(x_ref, xp_ref, g_ref, sc_ref, sh_ref, mu_ref, w1_ref, a1_ref, *o_refs, tm):
    def modulated(x):
        y = x * lax.rsqrt(jnp.mean(x * x, axis=-1, keepdims=True) + EPS) * g_ref[...]
        return y * (1.0 + sc_ref[0, 0:1, :]) + sh_ref[0, 0:1, :]

    h = modulated(x_ref[...])
    before = modulated(xp_ref[...])[7:8, :] * jnp.where(pl.program_id(0) > 0, 1.0, 0.0)
    prev = jnp.where(_iota((tm, 1), 0) == 0, before, pltpu.roll(h, 1, 0))
    xx = prev - h
    xr_o, xk_o, xv_o, xz_o, wl_o, al_o = o_refs
    for j, o_ref in ((0, xr_o), (2, xk_o), (3, xv_o), (5, xz_o)):
        o_ref[...] = (h + xx * mu_ref[j:j + 1, :]).astype(o_ref.dtype)
    xw = (h + xx * mu_ref[1:2, :]).astype(BF16)
    xa = (h + xx * mu_ref[4:5, :]).astype(BF16)
    wl_o[...] = jnp.tanh(_dot(xw, w1_ref[...])).astype(wl_o.dtype)
    al_o[...] = _dot(xa, a1_ref[...]).astype(al_o.dtype)


def _shift_lerp(x, gain, mod, layer, mu, w1, a1):
    s_len, d = x.shape
    tm = min(ROW_TM, s_len)
    row = pl.BlockSpec((tm, d), lambda i: (i, 0))
    low = pl.BlockSpec((tm, LORA_PAD), lambda i: (i, 0))
    down = pl.BlockSpec((d, LORA_PAD), lambda i: (0, 0))
    wide = jax.ShapeDtypeStruct((s_len, d), BF16)
    narrow = jax.ShapeDtypeStruct((s_len, LORA_PAD), BF16)
    return pl.pallas_call(
        functools.partial(_shift_lerp_kernel, tm=tm),
        out_shape=(wide, wide, wide, wide, narrow, narrow),
        grid=(s_len // tm,),
        in_specs=[
            row,
            pl.BlockSpec((8, d), lambda i: (jnp.maximum(i * (tm // 8) - 1, 0), 0)),
            pl.BlockSpec((1, d), lambda i: (0, 0)),
            pl.BlockSpec((1, 8, d), lambda i: (layer, 0, 1)),
            pl.BlockSpec((1, 8, d), lambda i: (layer, 0, 0)),
            pl.BlockSpec((6, d), lambda i: (0, 0)),
            down,
            down,
        ],
        out_specs=(row, row, row, row, low, low),
        compiler_params=_cparams("parallel"),
        name="rwkv_shift_lerp",
    )(x, x, gain.reshape(1, d), mod, mod, mu, w1, a1)


def _head_block_diag(value):
    shift = int(math.log2(RWKV_HEAD))
    same = (lax.shift_right_logical(_iota((LANES, LANES), 0), shift)
            == lax.shift_right_logical(_iota((LANES, LANES), 1), shift))
    return jnp.where(same, value, 0.0).astype(BF16)


def _head_sum(x, mat):
    hi = x.astype(BF16)
    lo = (x - hi.astype(F32)).astype(BF16)
    return _dot(hi, mat) + _dot(lo, mat)


def _rwkv_prep_kernel(k_ref, wa_ref, aa_ref, w2_ref, a2_ref, w0_ref, a0_ref, kk_ref, ka_ref, lw_o, kk_o, kka_o, k2_o):
    k = k_ref[...].astype(F32)
    lw_o[...] = -math.exp(-0.5) * jax.nn.sigmoid(w0_ref[...] + _dot(wa_ref[...], w2_ref[...]))
    a = jax.nn.sigmoid(a0_ref[...] + _dot(aa_ref[...], a2_ref[...]))
    kkr = k * kk_ref[...]
    ones = _head_block_diag(1.0)
    for j in range(k.shape[1] // LANES):
        sl = slice(j * LANES, (j + 1) * LANES)
        x = kkr[:, sl]
        kk = x / jnp.maximum(jnp.sqrt(_head_sum(x * x, ones)), 1e-12)
        kk_o[:, sl] = kk.astype(kk_o.dtype)
        kka_o[:, sl] = (kk * a[:, sl]).astype(kka_o.dtype)
    k2_o[...] = (k * (1.0 + (a - 1.0) * ka_ref[...])).astype(k2_o.dtype)


def _rwkv_prep(k, w_lora, a_lora, w2, a2, w0, a0, k_k, k_a):
    s_len, d = k.shape
    tm = min(ROW_TM, s_len)
    row = pl.BlockSpec((tm, d), lambda i: (i, 0))
    low = pl.BlockSpec((tm, LORA_PAD), lambda i: (i, 0))
    up = pl.BlockSpec((LORA_PAD, d), lambda i: (0, 0))
    vec = pl.BlockSpec((1, d), lambda i: (0, 0))
    return pl.pallas_call(
        _rwkv_prep_kernel,
        out_shape=(jax.ShapeDtypeStruct((s_len, d), F32),) + tuple(jax.ShapeDtypeStruct((s_len, d), BF16) for _ in range(3)),
        grid=(s_len // tm,),
        in_specs=[row, low, low, up, up, vec, vec, vec, vec],
        out_specs=(row, row, row, row),
        compiler_params=_cparams("parallel"),
        name="rwkv_prep",
    )(k, w_lora, a_lora, w2, a2, w0.reshape(1, d), a0.reshape(1, d), k_k.reshape(1, d), k_a.reshape(1, d))


def _rwkv_chunk_kernel(r_ref, lw_ref, k_ref, v_ref, kk_ref, kka_ref, y_ref, s_s, *, c, hb):
    n = RWKV_HEAD

    @pl.when(pl.program_id(1) == 0)
    def _():
        s_s[...] = jnp.zeros(s_s.shape, F32)

    rr = _iota((c, c), 0)
    cc = _iota((c, c), 1)
    lw = lw_ref[...]
    cum = _dot((rr >= cc).astype(F32), lw, HI)
    grow = jnp.exp(-cum)
    decay = jnp.exp(cum)
    total = decay[c - 1:c, :]
    q_t = kk_ref[...].astype(F32) * jnp.exp(cum - lw)
    p_t = -(kka_ref[...].astype(F32) * grow)
    k_t = k_ref[...].astype(F32) * grow
    r_t = r_ref[...].astype(F32) * decay
    p_end = p_t * total
    k_end = k_t * total
    r2 = _iota((2 * c, 2 * c), 0)
    c2 = _iota((2 * c, 2 * c), 1) & (c - 1)
    gram_mask = (((r2 < c) & (r2 > c2)) | ((r2 >= c) & ((r2 - c) >= c2))).astype(F32)
    eye = (rr == cc).astype(F32)
    same_block = [(lax.shift_right_logical(rr, s) == lax.shift_right_logical(cc, s)).astype(F32)
                  for s in range(3, int(math.log2(c)) + 1)]
    heads = range(hb)
    sls = [slice(i * n, (i + 1) * n) for i in heads]
    qr = [jnp.concatenate([q_t[:, sl], r_t[:, sl]], axis=0).astype(BF16) for sl in sls]
    pk = [jnp.concatenate([p_t[:, sl], k_t[:, sl]], axis=0).astype(BF16) for sl in sls]
    state = [s_s[i] for i in heads]
    from_state = [_dot_nt(qr[i], state[i].astype(BF16)) for i in heads]
    gram = [_dot_nt(qr[i], pk[i]) * gram_mask for i in heads]
    l_qp = [g[:c, :c] for g in gram]
    vb = [v_ref[:, sl] for sl in sls]
    from_v = [_dot(jnp.concatenate([gram[i][:c, c:], gram[i][c:, c:]], axis=0).astype(BF16), vb[i]) for i in heads]
    power = [l * same_block[0] for l in l_qp]
    inv = [eye + p for p in power]
    for _ in range(2):
        pb = [p.astype(BF16) for p in power]
        power = [_dot(p, p) for p in pb]
        inv = [x + _dot(x.astype(BF16), p.astype(BF16)) for x, p in zip(inv, power)]
    for lvl in range(1, len(same_block)):
        level_mask = same_block[lvl] - same_block[lvl - 1]
        ib = [x.astype(BF16) for x in inv]
        half = [_dot(ib[i], (l_qp[i] * level_mask).astype(BF16)).astype(BF16) for i in heads]
        inv = [inv[i] + _dot(half[i], ib[i]) for i in heads]
    u = [_dot(inv[i].astype(BF16), (from_state[i][:c] + from_v[i][:c]).astype(BF16)) for i in heads]
    ub = [x.astype(BF16) for x in u]
    for i in heads:
        y_ref[:, sls[i]] = from_state[i][c:] + _dot(gram[i][c:, :c].astype(BF16), ub[i]) + from_v[i][c:]
    for i in heads:
        uv = jnp.concatenate([ub[i], vb[i]], axis=0)
        pk_end = jnp.concatenate([p_end[:, sls[i]], k_end[:, sls[i]]], axis=0).astype(BF16)
        s_s[i] = state[i] * total[:, sls[i]] + _dot_tn(uv, pk_end)


def _rwkv_scan(r, lw, k2, v, kk, kka):
    s_len, d = r.shape
    c, hb = RWKV_CHUNK, RWKV_HB
    width = hb * RWKV_HEAD
    blk = pl.BlockSpec((c, width), lambda hg, ci: (ci, hg))
    return pl.pallas_call(
        functools.partial(_rwkv_chunk_kernel, c=c, hb=hb),
        out_shape=jax.ShapeDtypeStruct((s_len, d), F32),
        grid=(d // width, s_len // c),
        in_specs=[blk] * 6,
        out_specs=blk,
        scratch_shapes=[pltpu.VMEM((hb, RWKV_HEAD, RWKV_HEAD), F32)],
        compiler_params=_cparams("parallel", "arbitrary"),
        name="rwkv_chunk_scan",
    )(r, lw, k2, v, kk, kka)


def _rwkv_post_kernel(y_ref, r_ref, k2_ref, v_ref, z_ref, rk_ref, g_ref, b_ref, o_ref):
    avg = _head_block_diag(1.0 / RWKV_HEAD)
    ones = _head_block_diag(1.0)
    for j in range(y_ref.shape[1] // LANES):
        sl = slice(j * LANES, (j + 1) * LANES)
        y = y_ref[:, sl]
        dev = y - _head_sum(y, avg)
        var = _head_sum(dev * dev, avg)
        yn = dev * lax.rsqrt(var + LNX_EPS) * g_ref[:, sl] + b_ref[:, sl]
        rk = r_ref[:, sl].astype(F32) * k2_ref[:, sl].astype(F32) * rk_ref[:, sl]
        bonus = _head_sum(rk, ones) * v_ref[:, sl].astype(F32)
        o_ref[:, sl] = ((yn + bonus) * _silu(z_ref[:, sl].astype(F32))).astype(o_ref.dtype)


def _rwkv_post(y, r, k2, v, z, r_k, lnx_g, lnx_b):
    s_len, d = y.shape
    tm = min(ROW_TM, s_len)
    row = pl.BlockSpec((tm, d), lambda i: (i, 0))
    vec = pl.BlockSpec((1, d), lambda i: (0, 0))
    return pl.pallas_call(
        _rwkv_post_kernel,
        out_shape=jax.ShapeDtypeStruct((s_len, d), BF16),
        grid=(s_len // tm,),
        in_specs=[row, row, row, row, row, vec, vec, vec],
        out_specs=row,
        compiler_params=_cparams("parallel"),
        name="rwkv_post",
    )(y, r, k2, v, z, r_k.reshape(1, d), lnx_g.reshape(1, d), lnx_b.reshape(1, d))


def _rmsnorm_kernel(x_ref, g_ref, o_ref):
    x = x_ref[...].astype(F32)
    o_ref[...] = (x * lax.rsqrt(jnp.mean(x * x, axis=-1, keepdims=True) + EPS) * g_ref[...]).astype(o_ref.dtype)


def _rmsnorm(x, gain, out_dtype):
    s_len, d = x.shape
    tm = min(ROW_TM, s_len)
    return pl.pallas_call(
        _rmsnorm_kernel,
        out_shape=jax.ShapeDtypeStruct((s_len, d), out_dtype),
        grid=(s_len // tm,),
        in_specs=[pl.BlockSpec((tm, d), lambda i: (i, 0)), pl.BlockSpec((1, d), lambda i: (0, 0))],
        out_specs=pl.BlockSpec((tm, d), lambda i: (i, 0)),
        compiler_params=_cparams("parallel"),
        name="rmsnorm",
    )(x, gain.reshape(1, d))


def _rotate_half_cols(w):
    half = w.shape[1] // 2
    return jnp.concatenate([-w[:, half:], w[:, :half]], axis=1)


def _arrange_w_in(w_in):
    d = w_in.shape[0]
    q, kv, g, z, qa, kva, mlaz = jnp.split(w_in, [1024, 1792, 1840, 2864, 3376, 3952], axis=1)
    ckv, kpe = kva[:, :KV_LORA], kva[:, KV_LORA:]
    pad = jnp.zeros((d, IN0_COLS - C_G - g.shape[1]), w_in.dtype)
    return jnp.concatenate([q, z, mlaz, qa, ckv, kv, kpe, _rotate_half_cols(kpe), g, pad], axis=1).astype(BF16)


def _arrange_w_qb(w_qb):
    w = w_qb.reshape(Q_LORA, MLA_HEADS, QK_NOPE + QK_ROPE)
    nope = w[:, :, :QK_NOPE].reshape(Q_LORA, -1)
    pe = w[:, :, QK_NOPE:]
    half = QK_ROPE // 2
    rot = jnp.concatenate([-pe[:, :, half:], pe[:, :, :half]], axis=2)
    return jnp.concatenate([nope, pe.reshape(Q_LORA, -1), rot.reshape(Q_LORA, -1)], axis=1).astype(BF16)


def _arrange_w_kvb(w_kvb):
    w = w_kvb.reshape(KV_LORA, MLA_HEADS, QK_NOPE + V_HEAD)
    return jnp.concatenate([w[:, :, :QK_NOPE].reshape(KV_LORA, -1), w[:, :, QK_NOPE:].reshape(KV_LORA, -1)],
                           axis=1).astype(BF16)


def _rope_tables(s_len):
    inv = ROPE_THETA ** (-np.arange(0, QK_ROPE, 2, dtype=np.float32) / QK_ROPE)
    ang = np.arange(s_len, dtype=np.float32)[:, None] * inv[None].astype(np.float32)
    cos = np.concatenate([np.cos(ang), np.cos(ang)], axis=1).astype(np.float32)
    sin = np.concatenate([np.sin(ang), np.sin(ang)], axis=1).astype(np.float32)
    return jnp.asarray(cos), jnp.asarray(sin)


def _pad_cols(w, n):
    return jnp.pad(w, ((0, 0), (0, n - w.shape[1])))


def _pad_rows(w, n):
    return jnp.pad(w, ((0, n - w.shape[0]), (0, 0)))


def _attention_layer(x, mod, layer, gain, w_in, w_out, pe_k, w1_k, b1_k, w2_k, pe_v, w1_v, b1_v, w2_v,
                     qa_g, w_qb, kva_g, w_kvb):
    s_len, d = x.shape
    assert s_len % (CMP_STRIDE * LANES) == 0 and s_len // SLC_LEN <= SEL_LANES
    n_cmp = (s_len - CMP_LEN) // CMP_STRIDE + 1
    n_slc = s_len // SLC_LEN
    proj = _matmul(x, _arrange_w_in(w_in), out_dtype=F32, prologue="rms_mod", gain=gain, mod=mod, layer=layer,
                   tm=IN_PROJ_TM, name="in_proj")
    qfull = _matmul(proj, _arrange_w_qb(w_qb), out_dtype=BF16, a_col=C_QA // Q_LORA, k=Q_LORA, prologue="rms",
                    gain=qa_g, name="mla_q_proj")
    kv = _matmul(proj, _arrange_w_kvb(w_kvb), out_dtype=BF16, a_col=C_CKV // KV_LORA, k=KV_LORA, prologue="rms",
                 gain=kva_g, name="mla_kv_proj")
    cos, sin = _rope_tables(s_len)
    y_mla = _mla_attention(qfull, kv, proj, cos, sin)
    cl = CMP_LEN * NSA_DK
    kcvc = proj[:, C_KV:C_KV + 2 * NSA_GROUPS * NSA_DK].reshape(s_len, 2, NSA_GROUPS, NSA_DK)
    r = kcvc.transpose(1, 2, 0, 3).reshape(2 * NSA_GROUPS, s_len // CMP_STRIDE, CMP_STRIDE * NSA_DK)
    cmp_tokens = _compress_tokens(
        r,
        jnp.stack([w1_k.reshape(cl, -1), w1_v.reshape(cl, -1)]),
        jnp.stack([pe_k.reshape(1, cl), pe_v.reshape(1, cl)]),
        jnp.stack([b1_k.reshape(1, -1), b1_v.reshape(1, -1)]),
        jnp.stack([w2_k, w2_v]),
        n_cmp)
    o_cmp, sel, block_any = _nsa_compressed(proj, cmp_tokens, n_cmp, n_slc)
    o_win = _nsa_flash(proj, "win")
    y_nsa = _nsa_flash(proj, "slc", sel=sel, o_cmp=o_cmp, o_win=o_win, block_any=block_any)
    return _matmul(y_nsa, w_out.astype(BF16), a2=y_mla, out_dtype=F32, epilogue="residual", res=x, mod=mod,
                   layer=layer, name="attn_out_proj")


def _rwkv_layer(x, mod, layer, gain, mu, w_r, w_k, w_v, w_z, w_o, w0, w1, w2, a0, a1, a2, k_k, k_a, r_k,
                lnx_g, lnx_b, final_gain=None):
    xr, xk, xv, xz, w_lora, a_lora = _shift_lerp(x, gain, mod, layer, mu, _pad_cols(w1, LORA_PAD).astype(BF16),
                                                 _pad_cols(a1, LORA_PAD).astype(BF16))
    r = _matmul(xr, w_r.astype(BF16), out_dtype=BF16, name="rwkv_r")
    k = _matmul(xk, w_k.astype(BF16), out_dtype=BF16, name="rwkv_k")
    v = _matmul(xv, w_v.astype(BF16), out_dtype=BF16, name="rwkv_v")
    z = _matmul(xz, w_z.astype(BF16), out_dtype=BF16, name="rwkv_z")
    lw, kk, kka, k2 = _rwkv_prep(k, w_lora, a_lora, _pad_rows(w2, LORA_PAD).astype(BF16),
                                 _pad_rows(a2, LORA_PAD).astype(BF16), w0, a0, k_k, k_a)
    y = _rwkv_scan(r, lw, k2, v, kk, kka)
    y = _rwkv_post(y, r, k2, v, z, r_k, lnx_g, lnx_b)
    epilogue = "residual" if final_gain is None else "residual_rms"
    return _matmul(y, w_o.astype(BF16), out_dtype=F32, epilogue=epilogue, res=x, mod=mod, layer=layer,
                   final_gain=final_gain, tm=OUT_RMS_TM if final_gain is not None else None, name="rwkv_out_proj")


def kernel(x, c, norm_g, ada_w, ada_b, final_g, a_w_in, a_w_out, nsa_pe_k, nsa_w1_k, nsa_b1_k, nsa_w2_k, nsa_pe_v, nsa_w1_v, nsa_b1_v, nsa_w2_v, mla_qa_g, mla_w_qb, mla_kva_g, mla_w_kvb, r_mu, r_w_r, r_w_k, r_w_v, r_w_z, r_w_o, r_w0, r_w1, r_w2, r_a0, r_a1, r_a2, r_k_k, r_k_a, r_r_k, r_lnx_g, r_lnx_b):
    batch, s_len, d = x.shape
    depth = ada_w.shape[0]
    outs = []
    for b in range(batch):
        mod = _modulation(c[b:b + 1], ada_w, ada_b)
        xb = x[b]
        for i in range(depth):
            j = i // 2
            if i % 2 == 0:
                xb = _attention_layer(xb, mod, i, norm_g[i], a_w_in[j], a_w_out[j],
                                      nsa_pe_k[j], nsa_w1_k[j], nsa_b1_k[j], nsa_w2_k[j],
                                      nsa_pe_v[j], nsa_w1_v[j], nsa_b1_v[j], nsa_w2_v[j],
                                      mla_qa_g[j], mla_w_qb[j], mla_kva_g[j], mla_w_kvb[j])
            else:
                xb = _rwkv_layer(xb, mod, i, norm_g[i], r_mu[j], r_w_r[j], r_w_k[j], r_w_v[j], r_w_z[j], r_w_o[j],
                                 r_w0[j], r_w1[j], r_w2[j], r_a0[j], r_a1[j], r_a2[j], r_k_k[j], r_k_a[j],
                                 r_r_k[j], r_lnx_g[j], r_lnx_b[j], final_gain=final_g if i == depth - 1 else None)
        outs.append(xb if depth % 2 == 0 else _rmsnorm(xb, final_g, x.dtype))
    return outs[0].reshape(x.shape) if batch == 1 else jnp.stack(outs)
```
